```python
import jax, jax.numpy as jnp
from jax import lax
import numpy as np

D_MODEL = 1024
BATCH = 4
SEQ = 4096
DEPTH = 2

GRID_W = 64
CTX_LEN = 256
GLA_HEADS = 4
GLA_DK = 64
GLA_DV = 128
GLA_RANK = 16
GLA_TAU = 16.0
NA_HEADS = 8
NA_DH = 64
NA_WIN_R = 8
NA_WIN_C = 16
WA_HEADS = 8
WA_KV_HEADS = 2
WA_DH = 64
WA_RADIUS = 128
WA_BLOCK = 128
ROPE_BASE = 10000.0
HG_HEADS = 4
HG_DK = 128
HG_DV = 128
CHUNK = 64
N_BRANCH = 4
BRANCH_W = 512
N_GROUPS = 4
EXPERTS_PER_GROUP = 8
EXPERT_HIDDEN = 256
TOP_K = 2
LN_EPS = 1e-5

IN_COLS = (
    ("a_q", GLA_HEADS * GLA_DK), ("a_k", GLA_HEADS * GLA_DK), ("a_v", GLA_HEADS * GLA_DV),
    ("a_g", GLA_HEADS * GLA_DV), ("a_lr", 2 * GLA_RANK),
    ("b_q", NA_HEADS * NA_DH), ("b_k", NA_HEADS * NA_DH), ("b_v", NA_HEADS * NA_DH),
    ("c_q", WA_HEADS * WA_DH), ("c_k", WA_KV_HEADS * WA_DH), ("c_v", WA_KV_HEADS * WA_DH),
    ("d_q", HG_HEADS * HG_DK), ("d_f", 2 * HG_HEADS * HG_DK), ("d_i", HG_HEADS * HG_DV),
    ("d_g", HG_HEADS * HG_DV), ("gate", N_BRANCH * D_MODEL),
)
D_IN = sum(s for _, s in IN_COLS)

kernel_name = "hybrid_gla_natten_swa_hgrn2_hmoe_dit"

F32 = jnp.float32


def _split_cols(z):
    offs = np.cumsum([s for _, s in IN_COLS])[:-1].tolist()
    return dict(zip([n for n, _ in IN_COLS], jnp.split(z, offs, axis=-1)))


def _heads(t, n):
    b, l, w = t.shape
    return t.reshape(b, l, n, w // n).transpose(0, 2, 1, 3)


def _merge(t):
    b, h, l, d = t.shape
    return t.transpose(0, 2, 1, 3).reshape(b, l, h * d)


def _layer_norm(t, g, b):
    tf = t.astype(F32)
    mu = tf.mean(-1, keepdims=True)
    var = jnp.square(tf - mu).mean(-1, keepdims=True)
    return ((tf - mu) * lax.rsqrt(var + LN_EPS) * g + b).astype(t.dtype)


def _head_rms(o):
    of = o.astype(F32)
    return of * lax.rsqrt(jnp.mean(of * of, -1, keepdims=True) + 1e-6)


def _rope_2d(t):
    l, dh = t.shape[2], t.shape[3]
    half = dh // 2
    nf = half // 2
    pos = jnp.arange(l)
    inv = ROPE_BASE ** (-jnp.arange(nf, dtype=F32) / nf)

    def rot(u, p):
        ang = p.astype(F32)[:, None] * inv
        cos, sin = jnp.cos(ang), jnp.sin(ang)
        u1, u2 = u[..., :nf], u[..., nf:]
        return jnp.concatenate([u1 * cos - u2 * sin, u1 * sin + u2 * cos], -1)

    out = jnp.concatenate([rot(t[..., :half], pos // GRID_W), rot(t[..., half:], pos % GRID_W)], -1)
    return out.astype(t.dtype)


def _chunked_gated_scan(q, k, v, g, s0):
    b, h, l, dk = q.shape
    n = l // CHUNK

    def to_chunks(t):
        return jnp.moveaxis(t.reshape(b, h, n, CHUNK, t.shape[-1]), 2, 0)

    causal = jnp.tril(jnp.ones((CHUNK, CHUNK), bool))

    def step(s, inp):
        qi, ki, vi, gi = inp
        bc = jnp.cumsum(gi.astype(F32), axis=2)
        diff = bc[:, :, :, None, :] - bc[:, :, None, :, :]
        decay = jnp.exp(jnp.where(causal[:, :, None], diff, -jnp.inf))
        a = jnp.einsum("bhid,bhjd,bhijd->bhij", qi, ki, decay)
        o = jnp.einsum("bhij,bhjv->bhiv", a, vi) + jnp.einsum("bhid,bhdv->bhiv", qi * jnp.exp(bc), s)
        b_last = bc[:, :, -1:, :]
        s_new = jnp.exp(b_last[:, :, 0, :])[..., None] * s + jnp.einsum(
            "bhjd,bhjv->bhdv", ki * jnp.exp(b_last - bc), vi)
        return s_new, o

    s_fin, oc = lax.scan(step, s0, (to_chunks(q), to_chunks(k), to_chunks(v), to_chunks(g)))
    return jnp.moveaxis(oc, 0, 2).reshape(b, h, l, v.shape[-1]), s_fin


def _bidir_scan(q, k_fb, v, g_fb, qc, kc_fb, vc, gc_fb):
    b, h, _, dk = q.shape
    s0 = jnp.zeros((b, h, dk, v.shape[-1]), F32)
    flip = lambda t: jnp.flip(t, axis=2)
    oc_f, sc_f = _chunked_gated_scan(qc, kc_fb[0], vc, gc_fb[0], s0)
    oc_b, sc_b = _chunked_gated_scan(flip(qc), flip(kc_fb[1]), flip(vc), flip(gc_fb[1]), s0)
    o_f, _ = _chunked_gated_scan(q, k_fb[0], v, g_fb[0], sc_f)
    o_b, _ = _chunked_gated_scan(flip(q), flip(k_fb[1]), flip(v), flip(g_fb[1]), sc_b)
    return o_f + flip(o_b), oc_f + flip(oc_b)


def _gla_branch(p, pc, lr_w, lr_b, with_ctx):
    def prep(pp):
        q = _heads(pp["a_q"], GLA_HEADS) * (GLA_DK ** -0.5)
        k = _heads(pp["a_k"], GLA_HEADS)
        v = _heads(pp["a_v"], GLA_HEADS)
        lr = pp["a_lr"]
        lr = lr.reshape(lr.shape[:-1] + (2, GLA_RANK))
        logit = jnp.einsum("blzr,zrk->zblk", lr, lr_w) + lr_b[:, None, None, :]
        g = jax.nn.log_sigmoid(logit.astype(F32)) / GLA_TAU
        return q, (k, k), v, (_heads(g[0], GLA_HEADS), _heads(g[1], GLA_HEADS))

    o, oc = _bidir_scan(*prep(p), *prep(pc))

    def out(oo, pp):
        return (_merge(_head_rms(oo)) * jax.nn.silu(pp["a_g"].astype(F32))).astype(pp["a_g"].dtype)

    return out(o, p), (out(oc, pc) if with_ctx else None)


def _hgrn2_branch(p, pc, lb, with_ctx):
    def prep(pp):
        q = jax.nn.silu(_heads(pp["d_q"], HG_HEADS))
        f = pp["d_f"]
        f = f.reshape(f.shape[:-1] + (2, HG_HEADS * HG_DK)).astype(F32)
        log_f = jnp.logaddexp(jnp.log(lb), jnp.log1p(-lb) + jax.nn.log_sigmoid(f))
        k = -jnp.expm1(log_f)
        v = _heads(pp["d_i"], HG_HEADS)
        kk = (_heads(k[:, :, 0], HG_HEADS), _heads(k[:, :, 1], HG_HEADS))
        gg = (_heads(log_f[:, :, 0], HG_HEADS), _heads(log_f[:, :, 1], HG_HEADS))
        return q, kk, v, gg

    o, oc = _bidir_scan(*prep(p), *prep(pc))

    def out(oo, pp):
        return (_merge(_head_rms(oo)) * jax.nn.silu(pp["d_g"].astype(F32))).astype(pp["d_g"].dtype)

    return out(o, p), (out(oc, pc) if with_ctx else None)


def _ctx_attention(q, k, v, sink=None):
    b, hq, lc, dh = q.shape
    g = k.shape[1]
    qg = q.reshape(b, g, hq // g, lc, dh) * (dh ** -0.5)
    s = jnp.einsum("bgrid,bgjd->bgrij", qg, k).astype(F32)
    if sink is not None:
        s_sink = jnp.broadcast_to(sink.reshape(1, g, hq // g, 1, 1).astype(F32), s.shape[:-1] + (1,))
        s = jnp.concatenate([s, s_sink], -1)
    pr = jax.nn.softmax(s, -1)[..., :lc]
    o = jnp.einsum("bgrij,bgjd->bgrid", pr, v)
    return o.reshape(b, hq, lc, dh).astype(q.dtype)


def _neighborhood_attention(q, k, v, kc, vc, rpb):
    b, h, l, dh = q.shape
    rows = l // GRID_W
    wr = min(NA_WIN_R, rows)
    qg = q.reshape(b, h, rows, GRID_W, dh) * (dh ** -0.5)
    kg = k.reshape(b, h, rows, GRID_W, dh)
    vg = v.reshape(b, h, rows, GRID_W, dh)
    cols = np.arange(GRID_W)
    cs = np.clip(cols - NA_WIN_C // 2, 0, GRID_W - NA_WIN_C)
    col_idx = cs[:, None] + np.arange(NA_WIN_C)[None, :]
    col_bias_idx = col_idx - cols[:, None] + NA_WIN_C - 1
    rpb_cols = rpb[:, :, col_bias_idx]
    n_loc = wr * NA_WIN_C

    def one_row(r):
        rs = jnp.clip(r - wr // 2, 0, rows - wr)
        k_win = lax.dynamic_slice_in_dim(kg, rs, wr, axis=2)[:, :, :, col_idx]
        v_win = lax.dynamic_slice_in_dim(vg, rs, wr, axis=2)[:, :, :, col_idx]
        q_r = lax.dynamic_index_in_dim(qg, r, axis=2, keepdims=False)
        row_off = rs + jnp.arange(wr) - r + NA_WIN_R - 1
        bias = jnp.transpose(jnp.take(rpb_cols, row_off, axis=1), (0, 2, 1, 3))
        s_loc = (jnp.einsum("bhcd,bhrcwd->bhcrw", q_r, k_win) + bias[None]).astype(F32)
        s_ctx = jnp.einsum("bhcd,bhnd->bhcn", q_r, kc).astype(F32)
        pr = jax.nn.softmax(jnp.concatenate([s_loc.reshape(b, h, GRID_W, n_loc), s_ctx], -1), -1)
        p_loc = pr[..., :n_loc].reshape(b, h, GRID_W, wr, NA_WIN_C)
        o = jnp.einsum("bhcrw,bhrcwd->bhcd", p_loc, v_win) + jnp.einsum("bhcn,bhnd->bhcd", pr[..., n_loc:], vc)
        return o.astype(q.dtype)

    out = lax.map(one_row, jnp.arange(rows))
    return jnp.transpose(out, (1, 2, 0, 3, 4)).reshape(b, h, l, dh)


def _window_attention(q, k, v, kc, vc, sink):
    b, hq, l, dh = q.shape
    g = k.shape[1]
    rep = hq // g
    nb = l // WA_BLOCK
    lc = kc.shape[2]
    qb = q.reshape(b, g, rep, nb, WA_BLOCK, dh) * (dh ** -0.5)

    def band(t):
        tb = t.reshape(b, g, nb, WA_BLOCK, dh)
        tp = jnp.pad(tb, ((0, 0), (0, 0), (1, 1), (0, 0), (0, 0)))
        return jnp.concatenate([tp[:, :, :-2], tp[:, :, 1:-1], tp[:, :, 2:]], axis=3)

    kb, vb = band(k), band(v)
    qpos = np.arange(l).reshape(nb, WA_BLOCK)
    kpos = (np.arange(nb)[:, None] - 1) * WA_BLOCK + np.arange(3 * WA_BLOCK)[None, :]
    valid = ((np.abs(kpos[:, None, :] - qpos[:, :, None]) <= WA_RADIUS)
             & (kpos[:, None, :] >= 0) & (kpos[:, None, :] < l))
    s_loc = jnp.where(valid, jnp.einsum("bgrnid,bgnjd->bgrnij", qb, kb).astype(F32), -jnp.inf)
    s_ctx = jnp.einsum("bgrnid,bgcd->bgrnic", qb, kc).astype(F32)
    s_sink = jnp.broadcast_to(sink.reshape(1, g, rep, 1, 1, 1).astype(F32), s_loc.shape[:-1] + (1,))
    pr = jax.nn.softmax(jnp.concatenate([s_loc, s_ctx, s_sink], -1), -1)
    n_loc = 3 * WA_BLOCK
    o = (jnp.einsum("bgrnij,bgnjd->bgrnid", pr[..., :n_loc], vb)
         + jnp.einsum("bgrnic,bgcd->bgrnid", pr[..., n_loc:n_loc + lc], vc))
    return o.reshape(b, hq, l, dh).astype(q.dtype)


def _na_branch(p, pc, rpb, with_ctx):
    q, k, v = (_heads(p[n], NA_HEADS) for n in ("b_q", "b_k", "b_v"))
    kc, vc = _heads(pc["b_k"], NA_HEADS), _heads(pc["b_v"], NA_HEADS)
    o = _merge(_neighborhood_attention(q, k, v, kc, vc, rpb))
    oc = _merge(_ctx_attention(_heads(pc["b_q"], NA_HEADS), kc, vc)) if with_ctx else None
    return o, oc


def _wa_branch(p, pc, sink, with_ctx):
    q = _rope_2d(_heads(p["c_q"], WA_HEADS))
    k = _rope_2d(_heads(p["c_k"], WA_KV_HEADS))
    v = _heads(p["c_v"], WA_KV_HEADS)
    kc, vc = _heads(pc["c_k"], WA_KV_HEADS), _heads(pc["c_v"], WA_KV_HEADS)
    o = _merge(_window_attention(q, k, v, kc, vc, sink))
    oc = _merge(_ctx_attention(_heads(pc["c_q"], WA_HEADS), kc, vc, sink)) if with_ctx else None
    return o, oc


def _merge_branches(gate_pre, branches, w_br):
    gates = jax.nn.sigmoid(gate_pre.astype(F32)).astype(gate_pre.dtype)
    out = gates[..., :D_MODEL] * (branches[0] @ w_br[0])
    for i in range(1, N_BRANCH):
        out = out + gates[..., i * D_MODEL:(i + 1) * D_MODEL] * (branches[i] @ w_br[i])
    return out


def _hier_moe(h, w_grp, b_grp, w_er, b_er, w_gate, w_up, w_down):
    g_logits = (h @ w_grp + b_grp).astype(F32)
    g_top, g_idx = lax.top_k(g_logits, 1)
    g_weight = jnp.exp(g_top[..., 0] - jax.nn.logsumexp(g_logits, -1))
    e_logits = (h @ w_er + b_er).astype(F32)
    e_logits = e_logits.reshape(e_logits.shape[:-1] + (N_GROUPS, EXPERTS_PER_GROUP))
    g_onehot = jax.nn.one_hot(g_idx[..., 0], N_GROUPS, dtype=F32)
    e_in = jnp.einsum("...ge,...g->...e", e_logits, g_onehot)
    e_top, e_idx = lax.top_k(e_in, TOP_K)
    e_w = jax.nn.softmax(e_top, -1)
    local = jnp.sum(e_w[..., None] * jax.nn.one_hot(e_idx, EXPERTS_PER_GROUP, dtype=F32), axis=-2)
    out = None
    for gi in range(N_GROUPS):
        wg = (jnp.where(g_idx[..., 0] == gi, g_weight, 0.0)[..., None] * local).astype(h.dtype)
        hid = (jax.nn.silu(jnp.einsum("bld,edf->blef", h, w_gate[gi]))
               * jnp.einsum("bld,edf->blef", h, w_up[gi]))
        y = jnp.einsum("blef,efd->bld", hid * wg[..., None], w_down[gi])
        out = y if out is None else out + y
    return out


def setup_inputs(seed: int = 0) -> dict:
    key = jax.random.key(seed)
    ks = jax.random.split(key, 24)
    d = D_MODEL
    beta = (8.0 * DEPTH) ** -0.25
    nrm = lambda k, shape, scale: jax.random.normal(k, shape, F32) * scale
    n_exp = N_GROUPS * EXPERTS_PER_GROUP
    return {
        "x": nrm(ks[0], (BATCH, SEQ, d), 1.0),
        "c": nrm(ks[1], (BATCH, d), 1.0),
        "ctx": nrm(ks[2], (BATCH, CTX_LEN, d), 1.0),
        "c_ctx": nrm(ks[3], (d,), 1.0),
        "w_mod": nrm(ks[4], (DEPTH, d, 6 * d), 0.5 * d ** -0.5),
        "b_mod": nrm(ks[5], (DEPTH, 6 * d), 0.01),
        "w_in": nrm(ks[6], (DEPTH, d, D_IN), d ** -0.5),
        "gla_lr_w": nrm(ks[7], (DEPTH, 2, GLA_RANK, GLA_HEADS * GLA_DK), GLA_RANK ** -0.5),
        "gla_lr_b": nrm(ks[8], (DEPTH, 2, GLA_HEADS * GLA_DK), 0.1),
        "hg_gamma": nrm(ks[9], (2, DEPTH, HG_HEADS * HG_DK), 0.5),
        "na_rpb": nrm(ks[10], (DEPTH, NA_HEADS, 2 * NA_WIN_R - 1, 2 * NA_WIN_C - 1), 0.02),
        "wa_sink": nrm(ks[11], (DEPTH, WA_HEADS), 0.5),
        "w_branch": nrm(ks[12], (DEPTH, N_BRANCH, BRANCH_W, d), BRANCH_W ** -0.5),
        "w_out": nrm(ks[13], (DEPTH, d, d), beta * d ** -0.5),
        "ln_g": 1.0 + nrm(ks[14], (DEPTH, 2, d), 0.01),
        "ln_b": nrm(ks[15], (DEPTH, 2, d), 0.01),
        "moe_w_group": nrm(ks[16], (DEPTH, d, N_GROUPS), d ** -0.5),
        "moe_b_group": nrm(ks[17], (DEPTH, N_GROUPS), 0.01),
        "moe_w_expert": nrm(ks[18], (DEPTH, d, n_exp), d ** -0.5),
        "moe_b_expert": nrm(ks[19], (DEPTH, n_exp), 0.01),
        "moe_w_gate": nrm(ks[20], (DEPTH, N_GROUPS, EXPERTS_PER_GROUP, d, EXPERT_HIDDEN), d ** -0.5),
        "moe_w_up": nrm(ks[21], (DEPTH, N_GROUPS, EXPERTS_PER_GROUP, d, EXPERT_HIDDEN), d ** -0.5),
        "moe_w_down": nrm(ks[22], (DEPTH, N_GROUPS, EXPERTS_PER_GROUP, EXPERT_HIDDEN, d),
                          beta * EXPERT_HIDDEN ** -0.5),
    }


def reference(x, c, ctx, c_ctx, w_mod, b_mod, w_in, gla_lr_w, gla_lr_b, hg_gamma, na_rpb, wa_sink,
              w_branch, w_out, ln_g, ln_b, moe_w_group, moe_b_group, moe_w_expert, moe_b_expert,
              moe_w_gate, moe_w_up, moe_w_down):
    alpha = (2.0 * DEPTH) ** 0.25
    cum = jnp.cumsum(jax.nn.softmax(hg_gamma.astype(F32), axis=1), axis=1)
    lower_bounds = cum - cum[:, :1]
    sc = jax.nn.silu(c)
    scc = jax.nn.silu(c_ctx)
    cx = ctx
    for l in range(DEPTH):
        with_ctx = l < DEPTH - 1
        mod = jnp.split(sc @ w_mod[l] + b_mod[l], 6, axis=-1)
        modc = jnp.split(scc @ w_mod[l] + b_mod[l], 6, axis=-1)
        hx = x * (1.0 + mod[1][:, None]) + mod[0][:, None]
        hc = cx * (1.0 + modc[1]) + modc[0]
        p = _split_cols(hx @ w_in[l])
        pc = _split_cols(hc @ w_in[l])
        o_a, oc_a = _gla_branch(p, pc, gla_lr_w[l], gla_lr_b[l], with_ctx)
        o_b, oc_b = _na_branch(p, pc, na_rpb[l], with_ctx)
        o_c, oc_c = _wa_branch(p, pc, wa_sink[l], with_ctx)
        o_d, oc_d = _hgrn2_branch(p, pc, lower_bounds[:, l], with_ctx)
        mix = _merge_branches(p["gate"], (o_a, o_b, o_c, o_d), w_branch[l]) @ w_out[l]
        x = _layer_norm(alpha * x + mod[2][:, None] * mix, ln_g[l, 0], ln_b[l, 0])
        if with_ctx:
            mix_c = _merge_branches(pc["gate"], (oc_a, oc_b, oc_c, oc_d), w_branch[l]) @ w_out[l]
            cx = _layer_norm(alpha * cx + modc[2] * mix_c, ln_g[l, 0], ln_b[l, 0])
        moe_args = (moe_w_group[l], moe_b_group[l], moe_w_expert[l], moe_b_expert[l],
                    moe_w_gate[l], moe_w_up[l], moe_w_down[l])
        hx = x * (1.0 + mod[4][:, None]) + mod[3][:, None]
        x = _layer_norm(alpha * x + mod[5][:, None] * _hier_moe(hx, *moe_args), ln_g[l, 1], ln_b[l, 1])
        if with_ctx:
            hc = cx * (1.0 + modc[4]) + modc[3]
            cx = _layer_norm(alpha * cx + modc[5] * _hier_moe(hc, *moe_args), ln_g[l, 1], ln_b[l, 1])
    return x
```

```python
import functools
import math

import numpy as np
import jax
import jax.numpy as jnp
from jax import lax
from jax.experimental import pallas as pl
from jax.experimental.pallas import tpu as pltpu

F32 = jnp.float32
BF16 = jnp.bfloat16
HIGHEST = lax.Precision.HIGHEST

LANES = 128
VMEM_LIMIT = 56 * 1024 * 1024

D_MODEL = 1024
DEPTH = 2
GRID_W = 64
GLA_HEADS, GLA_DK, GLA_DV, GLA_RANK, GLA_TAU = 4, 64, 128, 16, 16.0
NA_HEADS, NA_DH, NA_WIN_R, NA_WIN_C = 8, 64, 8, 16
WA_HEADS, WA_KV_HEADS, WA_DH, WA_RADIUS, WA_BLOCK = 8, 2, 64, 128, 128
ROPE_BASE = 10000.0
HG_HEADS, HG_DK, HG_DV = 4, 128, 128
N_BRANCH, BRANCH_W = 4, 512
N_GROUPS, EXPERTS_PER_GROUP, EXPERT_HIDDEN = 4, 8, 256
N_EXPERTS = N_GROUPS * EXPERTS_PER_GROUP
LN_EPS = 1e-5
ALPHA = (2.0 * DEPTH) ** 0.25

_IN_COLS = (
    ("a_q", 256), ("a_k", 256), ("a_v", 512), ("a_g", 512), ("a_lr", 32),
    ("b_q", 512), ("b_k", 512), ("b_v", 512),
    ("c_q", 512), ("c_k", 128), ("c_v", 128),
    ("d_q", 512), ("d_f", 1024), ("d_i", 512), ("d_g", 512), ("gate", 4096),
)
_ORIG_OFF = {}
_o = 0
for _n, _s in _IN_COLS:
    _ORIG_OFF[_n] = _o
    _o += _s
D_IN = _o
_NEW_ORDER = ("gate", "a_q", "a_k", "a_v", "a_g", "b_q", "b_k", "b_v", "c_q", "c_k", "c_v",
              "d_q", "d_f", "d_i", "d_g", "a_lr")
_SIZES = dict(_IN_COLS)
OFF = {}
_o = 0
for _n in _NEW_ORDER:
    OFF[_n] = _o
    _o += _SIZES[_n]
PROJ_TN = 1536
NP = ((_o + PROJ_TN - 1) // PROJ_TN) * PROJ_TN

_WA_PERM = np.concatenate([np.concatenate([np.arange(j * 64, (j + 1) * 64), np.arange((j + 4) * 64, (j + 5) * 64)])
                           for j in range(4)])

SCAN_C = 128
SCAN_LEVELS = (64, 32, 16, 8, 4, 2, 1)


def _cparams(sem, vmem=VMEM_LIMIT):
    return pltpu.CompilerParams(dimension_semantics=sem, vmem_limit_bytes=vmem)


def _dot(a, b, **kw):
    return jnp.dot(a, b, preferred_element_type=F32, **kw)


def _dot_nt(a, b):
    return lax.dot_general(a, b, (((1,), (1,)), ((), ())), preferred_element_type=F32)


def _dot_tn(a, b):
    return lax.dot_general(a, b, (((0,), (0,)), ((), ())), preferred_element_type=F32)


def _sigmoid(x):
    return 1.0 / (1.0 + jnp.exp(-x))


def _silu(x):
    return x * _sigmoid(x)


def _log_sigmoid(x):
    return jnp.minimum(x, 0.0) - jnp.log1p(jnp.exp(-jnp.abs(x)))


def _layer_norm(z, g, b):
    mu = jnp.mean(z, axis=-1, keepdims=True)
    zc = z - mu
    var = jnp.mean(zc * zc, axis=-1, keepdims=True)
    return zc * lax.rsqrt(var + LN_EPS) * g + b


def _mod_kernel(c_ref, w_ref, b_ref, o_ref):
    c = c_ref[...]
    o_ref[0] = _dot(_silu(c), w_ref[0], precision=HIGHEST) + b_ref[0]


def _modulation(cc, w_mod, b_mod):
    depth, d, n = w_mod.shape
    tn = 1536
    return pl.pallas_call(
        _mod_kernel,
        grid=(depth, n // tn),
        in_specs=[pl.BlockSpec((8, d), lambda l, j: (0, 0)),
                  pl.BlockSpec((1, d, tn), lambda l, j: (l, 0, j)),
                  pl.BlockSpec((1, 1, tn), lambda l, j: (l, 0, j))],
        out_specs=pl.BlockSpec((1, 8, tn), lambda l, j: (l, 0, j)),
        out_shape=jax.ShapeDtypeStruct((depth, 8, n), F32),
        compiler_params=_cparams(("parallel", "parallel")),
        name="modulation",
    )(cc, w_mod, b_mod.reshape(depth, 1, n))


def _in_proj_kernel(x_ref, mod_ref, w_ref, o_ref, h_ref):
    @pl.when(pl.program_id(2) == 0)
    def _():
        m = mod_ref[0]
        h_ref[...] = (x_ref[0] * (1.0 + m[1:2]) + m[0:1]).astype(BF16)

    o_ref[0] = _dot(h_ref[...], w_ref[...])


def _in_proj(x, mod, w):
    b, l, d = x.shape
    tm = min(l, 1024)
    return pl.pallas_call(
        _in_proj_kernel,
        grid=(b, l // tm, NP // PROJ_TN),
        in_specs=[pl.BlockSpec((1, tm, d), lambda bi, i, j: (bi, i, 0)),
                  pl.BlockSpec((1, 8, d), lambda bi, i, j: (bi, 0, 0)),
                  pl.BlockSpec((d, PROJ_TN), lambda bi, i, j: (0, j))],
        out_specs=pl.BlockSpec((1, tm, PROJ_TN), lambda bi, i, j: (bi, i, j)),
        out_shape=jax.ShapeDtypeStruct((b, l, NP), F32),
        scratch_shapes=[pltpu.VMEM((tm, d), BF16)],
        compiler_params=_cparams(("parallel", "parallel", "arbitrary")),
        name="in_proj",
    )(x, mod, w)


def _scan_consts(reverse, nh):
    c = SCAN_C
    i = np.arange(c)[:, None]
    j = np.arange(c)[None, :]
    tri = (j >= i) if reverse else (j <= i)
    sels, masks = [], []
    for h in SCAN_LEVELS:
        same = (i // (2 * h)) == (j // (2 * h))
        base = (np.arange(c) // (2 * h)) * (2 * h)
        if reverse:
            pos = base + h
            m = same & ((i % (2 * h)) < h) & ((j % (2 * h)) >= h)
        else:
            pos = base + h - 1
            m = same & ((i % (2 * h)) >= h) & ((j % (2 * h)) < h)
        sels.append(j == pos[:, None])
        masks.append(m)
    masks.append(i == j)
    tri = jnp.asarray(tri, F32)
    sel = jnp.asarray(np.concatenate(sels, 0), BF16)
    msk = jnp.asarray(np.stack([np.tile(m, (nh, 1)) for m in masks]), F32)
    return tri, sel, msk


def _scan_chunk(qq, kk, g, v, st_ref, tri_ref, sel_ref, msk_ref, reverse, nh):
    c = SCAN_C
    bc = _dot(tri_ref[...], g, precision=HIGHEST)
    bhat = _dot(sel_ref[...], bc.astype(BF16))
    row = lax.broadcasted_iota(jnp.int32, (c, LANES), 0)
    lane = lax.broadcasted_iota(jnp.int32, (c, LANES), 1)

    def stack(t):
        if nh == 1:
            return t.astype(BF16)
        return jnp.concatenate([jnp.where(lane < 64, t, 0.0), jnp.where(lane >= 64, t, 0.0)], 0).astype(BF16)

    a_mat = jnp.zeros((nh * c, c), F32)
    for li, h in enumerate(SCAN_LEVELS):
        bnd = bhat[li * c:(li + 1) * c]
        upper = (row & (2 * h - 1)) >= h
        qrow = jnp.logical_not(upper) if reverse else upper
        eq = jnp.where(qrow, bc - bnd, 0.0)
        ek = jnp.where(qrow, 0.0, bnd - bc)
        a = _dot_nt(stack(qq * jnp.exp(eq)), (kk * jnp.exp(ek)).astype(BF16))
        a_mat = a_mat + jnp.where(msk_ref[li] > 0.0, a, 0.0)
    a = _dot_nt(stack(qq), kk.astype(BF16))
    a_mat = a_mat + jnp.where(msk_ref[len(SCAN_LEVELS)] > 0.0, a, 0.0)

    ab = a_mat.astype(BF16)
    vb = v.astype(BF16)
    o = jnp.concatenate([_dot(ab[hd * c:(hd + 1) * c], vb[:, hd * LANES:(hd + 1) * LANES]) for hd in range(nh)], 1)
    st = st_ref[...]
    o = o + _dot_nt((qq * jnp.exp(bc)).astype(BF16), st.astype(BF16))
    blast = bc[0:1] if reverse else bc[c - 1:c]
    ks = (kk * jnp.exp(blast - bc)).astype(BF16)
    upd = _dot_tn(vb, ks)
    if nh == 2:
        r2 = lax.broadcasted_iota(jnp.int32, (2 * LANES, LANES), 0)
        l2 = lax.broadcasted_iota(jnp.int32, (2 * LANES, LANES), 1)
        upd = jnp.where((r2 < LANES) == (l2 < 64), upd, 0.0)
    st_ref[...] = st * jnp.exp(blast) + upd
    return o


def _head_rms_gate(o, og, nh):
    outs = []
    for hd in range(nh):
        oh = o[:, hd * LANES:(hd + 1) * LANES]
        r = lax.rsqrt(jnp.mean(oh * oh, axis=-1, keepdims=True) + 1e-6)
        outs.append(oh * r * _silu(og[:, hd * LANES:(hd + 1) * LANES]))
    return outs[0] if nh == 1 else jnp.concatenate(outs, 1)


def _scan_kernel(*refs, mode, reverse, finalize, has_prev, t):
    nh = 2 if mode == "gla" else 1
    it = iter(refs)
    q_ref, k_ref, v_ref, og_ref = next(it), next(it), next(it), next(it)
    if mode == "gla":
        lr_ref, wlr_ref, blr_ref = next(it), next(it), next(it)
    else:
        lb_ref = next(it)
    init_ref, tri_ref, sel_ref, msk_ref = next(it), next(it), next(it), next(it)
    prev_ref = next(it) if has_prev else None
    o_ref, fin_ref, st_ref = next(it), next(it), next(it)

    @pl.when(pl.program_id(2) == 0)
    def _():
        st_ref[...] = init_ref[0, 0]

    nc = t // SCAN_C

    def body(ci, carry):
        cc = (nc - 1 - ci) if reverse else ci
        sl = pl.ds(pl.multiple_of(cc * SCAN_C, SCAN_C), SCAN_C)
        v = v_ref[0, sl, :]
        if mode == "gla":
            qq = q_ref[0, sl, :] * (GLA_DK ** -0.5)
            kk = k_ref[0, sl, :]
            logit = _dot(lr_ref[0, sl, :].astype(BF16), wlr_ref[0, 0]) + blr_ref[0, 0]
            g = _log_sigmoid(logit) * (1.0 / GLA_TAU)
        else:
            qq = _silu(q_ref[0, sl, :])
            lb = lb_ref[0]
            ls = jnp.log1p(-lb) + _log_sigmoid(k_ref[0, sl, :])
            llb = jnp.log(lb)
            g = jnp.maximum(llb, ls) + jnp.log1p(jnp.exp(-jnp.abs(llb - ls)))
            kk = 1.0 - jnp.exp(g)
        o = _scan_chunk(qq, kk, g, v, st_ref, tri_ref, sel_ref, msk_ref, reverse, nh)
        if has_prev:
            o = o + prev_ref[0, sl, :]
        if finalize:
            o = _head_rms_gate(o, og_ref[0, sl, :], nh)
        o_ref[0, sl, :] = o
        return carry

    lax.fori_loop(0, nc, body, 0)

    @pl.when(pl.program_id(2) == pl.num_programs(2) - 1)
    def _():
        fin_ref[0, 0] = st_ref[...]


def _scan_call(mode, p, init, prev, aux, *, reverse, finalize):
    b, l, _ = p.shape
    nh = 2 if mode == "gla" else 1
    ninst = 2 if mode == "gla" else HG_HEADS
    t = min(l, 512)
    nb = l // t
    w = LANES * nh
    z = 1 if reverse else 0

    def seq(i):
        return (nb - 1 - i) if reverse else i

    def col(name, width, extra=0):
        base = (OFF[name] + extra) // width
        return pl.BlockSpec((1, t, width), lambda bi, n, i: (bi, seq(i), base + n))

    if mode == "gla":
        in_specs = [col("a_q", LANES), col("a_k", LANES), col("a_v", w), col("a_g", w),
                    pl.BlockSpec((1, t, LANES), lambda bi, n, i: (bi, seq(i), OFF["a_lr"] // LANES)),
                    pl.BlockSpec((1, 1, LANES, LANES), lambda bi, n, i: (z, n, 0, 0)),
                    pl.BlockSpec((1, 1, 1, LANES), lambda bi, n, i: (z, n, 0, 0))]
        args = [p, p, p, p, p, aux[0], aux[1]]
    else:
        in_specs = [col("d_q", LANES), col("d_f", LANES, z * HG_HEADS * HG_DK), col("d_i", LANES), col("d_g", LANES),
                    pl.BlockSpec((1, 1, LANES), lambda bi, n, i: (z, 0, n))]
        args = [p, p, p, p, aux[0]]
    tri, sel, msk = _scan_consts(reverse, nh)
    in_specs += [pl.BlockSpec((1, 1, w, LANES), lambda bi, n, i: (bi, n, 0, 0)),
                 pl.BlockSpec(tri.shape, lambda bi, n, i: (0, 0)),
                 pl.BlockSpec(sel.shape, lambda bi, n, i: (0, 0)),
                 pl.BlockSpec(msk.shape, lambda bi, n, i: (0, 0, 0))]
    args += [init, tri, sel, msk]
    if prev is not None:
        in_specs.append(pl.BlockSpec((1, t, w), lambda bi, n, i: (bi, seq(i), n)))
        args.append(prev)
    kern = functools.partial(_scan_kernel, mode=mode, reverse=reverse, finalize=finalize,
                             has_prev=prev is not None, t=t)
    return pl.pallas_call(
        kern,
        grid=(b, ninst, nb),
        in_specs=in_specs,
        out_specs=[pl.BlockSpec((1, t, w), lambda bi, n, i: (bi, seq(i), n)),
                   pl.BlockSpec((1, 1, w, LANES), lambda bi, n, i: (bi, n, 0, 0))],
        out_shape=[jax.ShapeDtypeStruct((b, l, ninst * w), F32),
                   jax.ShapeDtypeStruct((b, ninst, w, LANES), F32)],
        scratch_shapes=[pltpu.VMEM((w, LANES), F32)],
        compiler_params=_cparams(("parallel", "parallel", "arbitrary")),
        name=f"scan_{mode}_{'bwd' if reverse else 'fwd'}",
    )(*args)


def _recurrent_branch(mode, p, pc, aux, with_ctx):
    b = p.shape[0]
    nh = 2 if mode == "gla" else 1
    ninst = 2 if mode == "gla" else HG_HEADS
    zero = jnp.zeros((b, ninst, LANES * nh, LANES), F32)
    oc_f, sc_f = _scan_call(mode, pc, zero, None, aux, reverse=False, finalize=False)
    o_f, _ = _scan_call(mode, p, sc_f, None, aux, reverse=False, finalize=False)
    oc, sc_b = _scan_call(mode, pc, zero, oc_f, aux, reverse=True, finalize=True)
    o, _ = _scan_call(mode, p, sc_b, o_f, aux, reverse=True, finalize=True)
    return o, (oc if with_ctx else None)


def _na_bias_table(rpb):
    cols = np.arange(GRID_W)
    cs = np.clip(cols - NA_WIN_C // 2, 0, GRID_W - NA_WIN_C)
    kc = np.arange(GRID_W)
    valid = (kc[None, :] >= cs[:, None]) & (kc[None, :] < cs[:, None] + NA_WIN_C)
    col_idx = np.clip(kc[None, :] - cols[:, None] + NA_WIN_C - 1, 0, 2 * NA_WIN_C - 2)
    idx = np.arange(NA_WIN_R)
    krl = np.arange(NA_WIN_R)
    row_idx = krl[None, :] - idx[:, None] + NA_WIN_R - 1
    t = rpb[:, row_idx]
    t = t[:, :, :, col_idx]
    t = jnp.where(valid[None, None, None], t, -jnp.inf)
    t = jnp.transpose(t, (0, 1, 3, 2, 4))
    return t.reshape(NA_HEADS, NA_WIN_R, GRID_W, NA_WIN_R * GRID_W).astype(F32)


def _softmax_av(parts, extra=None):
    m = functools.reduce(jnp.maximum, [jnp.max(s, axis=-1, keepdims=True) for s, _ in parts])
    if extra is not None:
        m = jnp.maximum(m, extra)
    den = 0.0
    acc = 0.0
    for s, v in parts:
        e = jnp.exp(s - m)
        den = den + jnp.sum(e, axis=-1, keepdims=True)
        acc = acc + _dot(e.astype(BF16), v)
    if extra is not None:
        den = den + jnp.exp(extra - m)
    return acc / den


def _na_kernel(q_ref, k_ref, v_ref, kc_ref, vc_ref, bias_ref, o_ref, *, rows):
    kc = kc_ref[0].astype(BF16)
    vc = vc_ref[0].astype(BF16)
    lane = lax.broadcasted_iota(jnp.int32, (GRID_W, LANES), 1)
    wr = NA_WIN_R

    def body(r, carry):
        rs = jnp.clip(r - wr // 2, 0, rows - wr)
        idx = r - rs
        qs = pl.ds(pl.multiple_of(r * GRID_W, GRID_W), GRID_W)
        ks = pl.ds(pl.multiple_of(rs * GRID_W, GRID_W), wr * GRID_W)
        q = q_ref[0, qs, :] * (NA_DH ** -0.5)
        kw = k_ref[0, ks, :].astype(BF16)
        vw = v_ref[0, ks, :].astype(BF16)
        outs = []
        for hd in range(2):
            qm = jnp.where((lane < 64) if hd == 0 else (lane >= 64), q, 0.0).astype(BF16)
            s_loc = _dot_nt(qm, kw) + bias_ref[hd, idx]
            s_ctx = _dot_nt(qm, kc)
            outs.append(_softmax_av([(s_loc, vw), (s_ctx, vc)]))
        o_ref[0, qs, :] = jnp.where(lane < 64, outs[0], outs[1])
        return carry

    lax.fori_loop(0, rows, body, 0)


def _na_attention(p, pc, bias):
    b, l, _ = p.shape
    lc = pc.shape[1]
    rows = l // GRID_W
    npair = NA_HEADS // 2

    def col(name):
        base = OFF[name] // LANES
        return pl.BlockSpec((1, l, LANES), lambda bi, n: (bi, 0, base + n))

    def ccol(name):
        base = OFF[name] // LANES
        return pl.BlockSpec((1, lc, LANES), lambda bi, n: (bi, 0, base + n))

    return pl.pallas_call(
        functools.partial(_na_kernel, rows=rows),
        grid=(b, npair),
        in_specs=[col("b_q"), col("b_k"), col("b_v"), ccol("b_k"), ccol("b_v"),
                  pl.BlockSpec((2, NA_WIN_R, GRID_W, NA_WIN_R * GRID_W), lambda bi, n: (n, 0, 0, 0))],
        out_specs=pl.BlockSpec((1, l, LANES), lambda bi, n: (bi, 0, n)),
        out_shape=jax.ShapeDtypeStruct((b, l, NA_HEADS * NA_DH), F32),
        compiler_params=_cparams(("parallel", "parallel")),
        name="na_attention",
    )(p, p, p, pc, pc, bias)


def _ctx_attn_kernel(q_ref, k_ref, v_ref, sink_ref, o_ref, *, use_sink):
    lc = q_ref.shape[1]
    lane = lax.broadcasted_iota(jnp.int32, (lc, LANES), 1)
    q = q_ref[0] * (NA_DH ** -0.5)
    kb = k_ref[0].astype(BF16)
    vb = v_ref[0].astype(BF16)
    outs = []
    for hd in range(2):
        qm = jnp.where((lane < 64) if hd == 0 else (lane >= 64), q, 0.0).astype(BF16)
        s = _dot_nt(qm, kb)
        extra = sink_ref[0, hd][:, 0:1] if use_sink else None
        outs.append(_softmax_av([(s, vb)], extra))
    o_ref[0] = jnp.where(lane < 64, outs[0], outs[1])


def _ctx_attention(pc, qname, kname, vname, kv_per_q, sink_tab):
    b, lc, _ = pc.shape
    nblk = 4
    use_sink = sink_tab is not None
    if sink_tab is None:
        sink_tab = jnp.zeros((nblk, 2, lc, LANES), F32)
    qb, kb, vb = OFF[qname] // LANES, OFF[kname] // LANES, OFF[vname] // LANES
    kmul = 1 if kv_per_q else 0
    return pl.pallas_call(
        functools.partial(_ctx_attn_kernel, use_sink=use_sink),
        grid=(b, nblk),
        in_specs=[pl.BlockSpec((1, lc, LANES), lambda bi, n: (bi, 0, qb + n)),
                  pl.BlockSpec((1, lc, LANES), lambda bi, n: (bi, 0, kb + kmul * n)),
                  pl.BlockSpec((1, lc, LANES), lambda bi, n: (bi, 0, vb + kmul * n)),
                  pl.BlockSpec((1, 2, lc, LANES), lambda bi, n: (n, 0, 0, 0))],
        out_specs=pl.BlockSpec((1, lc, LANES), lambda bi, n: (bi, 0, n)),
        out_shape=jax.ShapeDtypeStruct((b, lc, nblk * LANES), F32),
        compiler_params=_cparams(("parallel", "parallel")),
        name="ctx_attention",
    )(pc, pc, pc, sink_tab)


def _rope_tables(l):
    nf = WA_DH // 4
    pos = np.arange(l)
    inv = ROPE_BASE ** (-np.arange(nf, dtype=np.float32) / nf)
    d = np.arange(WA_DH)
    p = np.where(d[None, :] < WA_DH // 2, (pos // GRID_W)[:, None], (pos % GRID_W)[:, None]).astype(np.float32)
    ang = p * inv[d % nf][None, :].astype(np.float32)
    cos, sin = np.cos(ang), np.sin(ang)
    first = (d % (2 * nf)) < nf
    ta = np.where(first[None, :], -sin, 0.0)
    tb = np.where(first[None, :], 0.0, sin)
    tile = lambda t: jnp.asarray(np.tile(t, (1, LANES // WA_DH)), F32)
    return tile(cos), tile(ta), tile(tb)


def _rope_kernel(q_ref, k_ref, c_ref, a_ref, b_ref, qo_ref, ko_ref):
    c, a, b = c_ref[...], a_ref[...], b_ref[...]
    nf = WA_DH // 4

    def rot(u):
        return u * c + pltpu.roll(u, LANES - nf, 1) * a + pltpu.roll(u, nf, 1) * b

    for j in range(q_ref.shape[2] // LANES):
        qo_ref[0, :, j * LANES:(j + 1) * LANES] = rot(q_ref[0, :, j * LANES:(j + 1) * LANES])
    ko_ref[0] = rot(k_ref[0])


def _rope(p, tabs):
    b, l, _ = p.shape
    t = min(l, 512)
    qw = WA_HEADS * WA_DH
    tspec = pl.BlockSpec((t, LANES), lambda bi, i: (i, 0))
    return pl.pallas_call(
        _rope_kernel,
        grid=(b, l // t),
        in_specs=[pl.BlockSpec((1, t, qw), lambda bi, i: (bi, i, OFF["c_q"] // qw)),
                  pl.BlockSpec((1, t, LANES), lambda bi, i: (bi, i, OFF["c_k"] // LANES)),
                  tspec, tspec, tspec],
        out_specs=[pl.BlockSpec((1, t, qw), lambda bi, i: (bi, i, 0)),
                   pl.BlockSpec((1, t, LANES), lambda bi, i: (bi, i, 0))],
        out_shape=[jax.ShapeDtypeStruct((b, l, qw), F32), jax.ShapeDtypeStruct((b, l, LANES), F32)],
        compiler_params=_cparams(("parallel", "parallel")),
        name="rope",
    )(p, p, *tabs)


def _wa_kernel(q_ref, k_ref, v_ref, kc_ref, vc_ref, sink_ref, o_ref, *, l):
    n = pl.program_id(1)
    blk = WA_BLOCK
    span = 3 * blk
    start = jnp.clip((n - 1) * blk, 0, l - span)
    ks = pl.ds(pl.multiple_of(start, blk), span)
    kw = k_ref[0, ks, :].astype(BF16)
    vw = v_ref[0, ks, :].astype(BF16)
    kc = kc_ref[0].astype(BF16)
    vc = vc_ref[0].astype(BF16)
    ngrp = WA_HEADS // 2
    lane = lax.broadcasted_iota(jnp.int32, (blk, LANES), 1)
    ii = lax.broadcasted_iota(jnp.int32, (ngrp * blk, span), 0) & (blk - 1)
    jj = lax.broadcasted_iota(jnp.int32, (ngrp * blk, span), 1)
    dist = (jj + start) - (ii + n * blk)
    valid = jnp.abs(dist) <= WA_RADIUS
    halves = []
    for half in range(2):
        keep = (lane < 64) if half == 0 else (lane >= 64)
        qs = jnp.concatenate(
            [jnp.where(keep, q_ref[0, :, j * LANES:(j + 1) * LANES] * (WA_DH ** -0.5), 0.0) for j in range(ngrp)],
            0).astype(BF16)
        s_loc = jnp.where(valid, _dot_nt(qs, kw), -jnp.inf)
        s_ctx = _dot_nt(qs, kc)
        halves.append(_softmax_av([(s_loc, vw), (s_ctx, vc)], sink_ref[half][:, 0:1]))
    for j in range(ngrp):
        o_ref[0, :, j * LANES:(j + 1) * LANES] = jnp.where(lane < 64, halves[0][j * blk:(j + 1) * blk],
                                                           halves[1][j * blk:(j + 1) * blk])


def _wa_attention(qr, kr, p, pc, sink_rows):
    b, l, qw = qr.shape
    lc = pc.shape[1]
    return pl.pallas_call(
        functools.partial(_wa_kernel, l=l),
        grid=(b, l // WA_BLOCK),
        in_specs=[pl.BlockSpec((1, WA_BLOCK, qw), lambda bi, n: (bi, n, 0)),
                  pl.BlockSpec((1, l, LANES), lambda bi, n: (bi, 0, 0)),
                  pl.BlockSpec((1, l, LANES), lambda bi, n: (bi, 0, OFF["c_v"] // LANES)),
                  pl.BlockSpec((1, lc, LANES), lambda bi, n: (bi, 0, OFF["c_k"] // LANES)),
                  pl.BlockSpec((1, lc, LANES), lambda bi, n: (bi, 0, OFF["c_v"] // LANES)),
                  pl.BlockSpec(sink_rows.shape, lambda bi, n: (0, 0, 0))],
        out_specs=pl.BlockSpec((1, WA_BLOCK, qw), lambda bi, n: (bi, n, 0)),
        out_shape=jax.ShapeDtypeStruct((b, l, qw), F32),
        compiler_params=_cparams(("parallel", "arbitrary")),
        name="wa_attention",
    )(qr, kr, p, pc, pc, sink_rows)


def _merge_kernel(oa_ref, ob_ref, oc_ref, od_ref, gate_ref, x_ref, mod_ref, wbr_ref, wout_ref, lng_ref, lnb_ref,
                  o_ref):
    acc = 0.0
    for i, br in enumerate((oa_ref, ob_ref, oc_ref, od_ref)):
        y = _dot(br[0].astype(BF16), wbr_ref[i])
        acc = acc + _sigmoid(gate_ref[0, :, i * D_MODEL:(i + 1) * D_MODEL]) * y
    mix = _dot(acc.astype(BF16), wout_ref[...])
    m = mod_ref[0]
    z = ALPHA * x_ref[0] + m[2:3] * mix
    o_ref[0] = _layer_norm(z, lng_ref[...], lnb_ref[...])


def _merge_out_ln(branches, p, x, mod, wbr, wout, lng, lnb):
    b, l, d = x.shape
    tm = min(l, 256)
    gw = N_BRANCH * D_MODEL
    bspec = pl.BlockSpec((1, tm, BRANCH_W), lambda bi, i: (bi, i, 0))
    return pl.pallas_call(
        _merge_kernel,
        grid=(b, l // tm),
        in_specs=[bspec, bspec, bspec, bspec,
                  pl.BlockSpec((1, tm, gw), lambda bi, i: (bi, i, OFF["gate"] // gw)),
                  pl.BlockSpec((1, tm, d), lambda bi, i: (bi, i, 0)),
                  pl.BlockSpec((1, 8, d), lambda bi, i: (bi, 0, 0)),
                  pl.BlockSpec((N_BRANCH, BRANCH_W, d), lambda bi, i: (0, 0, 0)),
                  pl.BlockSpec((d, d), lambda bi, i: (0, 0)),
                  pl.BlockSpec((1, d), lambda bi, i: (0, 0)),
                  pl.BlockSpec((1, d), lambda bi, i: (0, 0))],
        out_specs=pl.BlockSpec((1, tm, d), lambda bi, i: (bi, i, 0)),
        out_shape=jax.ShapeDtypeStruct((b, l, d), F32),
        compiler_params=_cparams(("parallel", "parallel")),
        name="merge_out_ln",
    )(*branches, p, x, mod, wbr, wout, lng, lnb)


def _route(h, wg_ref, bg_ref, we_ref, be_ref):
    tm = h.shape[0]
    lane = lax.broadcasted_iota(jnp.int32, (tm, LANES), 1)
    ninf = -jnp.inf
    gl = jnp.where(lane < N_GROUPS, _dot(h, wg_ref[...], precision=HIGHEST) + bg_ref[...], ninf)
    gmax = jnp.max(gl, axis=-1, keepdims=True)
    gidx = jnp.min(jnp.where(gl == gmax, lane, LANES), axis=-1, keepdims=True)
    gw = 1.0 / jnp.sum(jnp.exp(gl - gmax), axis=-1, keepdims=True)
    el = _dot(h, we_ref[...], precision=HIGHEST) + be_ref[...]
    ingrp = jnp.logical_and(lane < N_EXPERTS, lax.shift_right_logical(lane, 3) == gidx)
    e1 = jnp.where(ingrp, el, ninf)
    m1 = jnp.max(e1, axis=-1, keepdims=True)
    i1 = jnp.min(jnp.where(e1 == m1, lane, LANES), axis=-1, keepdims=True)
    e2 = jnp.where(lane == i1, ninf, e1)
    m2 = jnp.max(e2, axis=-1, keepdims=True)
    i2 = jnp.min(jnp.where(e2 == m2, lane, LANES), axis=-1, keepdims=True)
    t = jnp.exp(m2 - m1)
    w1 = 1.0 / (1.0 + t)
    w2 = t / (1.0 + t)
    return gw * (jnp.where(lane == i1, w1, 0.0) + jnp.where(lane == i2, w2, 0.0))


def _moe_kernel(x_ref, mod_ref, wg_ref, bg_ref, we_ref, be_ref, wgu_ref, wd_ref, lng_ref, lnb_ref, o_ref,
                h_ref, wts_ref, acc_ref):
    e = pl.program_id(2)
    m = mod_ref[0]

    @pl.when(e == 0)
    def _():
        h = x_ref[0] * (1.0 + m[4:5]) + m[3:4]
        h_ref[...] = h.astype(BF16)
        wts_ref[...] = _route(h, wg_ref, bg_ref, we_ref, be_ref)
        acc_ref[...] = jnp.zeros_like(acc_ref)

    lane = lax.broadcasted_iota(jnp.int32, wts_ref.shape, 1)
    wcol = jnp.sum(jnp.where(lane == e, wts_ref[...], 0.0), axis=-1, keepdims=True)
    gu = _dot(h_ref[...], wgu_ref[0])
    hid = _silu(gu[:, :EXPERT_HIDDEN]) * gu[:, EXPERT_HIDDEN:] * wcol
    acc_ref[...] += _dot(hid.astype(BF16), wd_ref[0])

    @pl.when(e == pl.num_programs(2) - 1)
    def _():
        z = ALPHA * x_ref[0] + m[5:6] * acc_ref[...]
        o_ref[0] = _layer_norm(z, lng_ref[...], lnb_ref[...])


def _moe_ln(x, mod, wg, bg, we, be, wgu, wd, lng, lnb):
    b, l, d = x.shape
    tm = min(l, 1024)
    cst = lambda shape: pl.BlockSpec(shape, lambda bi, i, e: tuple(0 for _ in shape))
    return pl.pallas_call(
        _moe_kernel,
        grid=(b, l // tm, N_EXPERTS),
        in_specs=[pl.BlockSpec((1, tm, d), lambda bi, i, e: (bi, i, 0)),
                  pl.BlockSpec((1, 8, d), lambda bi, i, e: (bi, 0, 0)),
                  cst((d, LANES)), cst((1, LANES)), cst((d, LANES)), cst((1, LANES)),
                  pl.BlockSpec((1, d, 2 * EXPERT_HIDDEN), lambda bi, i, e: (e, 0, 0)),
                  pl.BlockSpec((1, EXPERT_HIDDEN, d), lambda bi, i, e: (e, 0, 0)),
                  cst((1, d)), cst((1, d))],
        out_specs=pl.BlockSpec((1, tm, d), lambda bi, i, e: (bi, i, 0)),
        out_shape=jax.ShapeDtypeStruct((b, l, d), F32),
        scratch_shapes=[pltpu.VMEM((tm, d), BF16), pltpu.VMEM((tm, LANES), F32), pltpu.VMEM((tm, d), F32)],
        compiler_params=_cparams(("parallel", "parallel", "arbitrary")),
        name="moe_ln",
    )(x, mod, wg, bg, we, be, wgu, wd, lng, lnb)


def _pad_cols(w, n):
    return jnp.pad(w, ((0, 0), (0, n - w.shape[1])))


def _prep_w_in(w):
    segs = []
    for name in _NEW_ORDER:
        s = w[:, _ORIG_OFF[name]:_ORIG_OFF[name] + _SIZES[name]]
        if name == "c_q":
            s = s[:, _WA_PERM]
        segs.append(s)
    return _pad_cols(jnp.concatenate(segs, 1), NP).astype(BF16)


def _prep_gla_gate(lr_w, lr_b):
    w = jnp.zeros((2, 2, LANES, LANES), F32)
    for z in range(2):
        for n in range(2):
            w = w.at[z, n, z * GLA_RANK:(z + 1) * GLA_RANK, :].set(lr_w[z][:, n * LANES:(n + 1) * LANES])
    return w.astype(BF16), lr_b.reshape(2, 2, 1, LANES)


def _sink_rows(sink, rows_per_head):
    s = sink.reshape(2, 4)
    return jnp.broadcast_to(s[:, :, None, None], (2, 4, rows_per_head, LANES)).reshape(2, 4 * rows_per_head, LANES)


def kernel(x, c, ctx, c_ctx, w_mod, b_mod, w_in, gla_lr_w, gla_lr_b, hg_gamma, na_rpb, wa_sink, w_branch, w_out,
           ln_g, ln_b, moe_w_group, moe_b_group, moe_w_expert, moe_b_expert, moe_w_gate, moe_w_up, moe_w_down):
    b, l, d = x.shape
    lc = ctx.shape[1]
    assert d == D_MODEL and b + 1 <= 8 and l % 512 == 0 and l >= 3 * WA_BLOCK and lc % SCAN_C == 0
    assert l // GRID_W >= NA_WIN_R

    cc = jnp.concatenate([c, c_ctx[None], jnp.zeros((8 - b - 1, d), F32)], 0)
    mods = _modulation(cc, w_mod, b_mod)
    cum = jnp.cumsum(jax.nn.softmax(hg_gamma.astype(F32), axis=1), axis=1)
    lower_bounds = cum - cum[:, :1]
    rope_tabs = _rope_tables(l)

    cx = ctx
    for li in range(DEPTH):
        with_ctx = li < DEPTH - 1
        mod6 = mods[li].reshape(8, 6, d)
        mod = jnp.pad(mod6[:b], ((0, 0), (0, 2), (0, 0)))
        modc = jnp.broadcast_to(jnp.pad(mod6[b], ((0, 2), (0, 0)))[None], (b, 8, d))
        w_in_p = _prep_w_in(w_in[li])
        p = _in_proj(x, mod, w_in_p)
        pc = _in_proj(cx, modc, w_in_p)

        gla_aux = _prep_gla_gate(gla_lr_w[li], gla_lr_b[li])
        o_a, oc_a = _recurrent_branch("gla", p, pc, gla_aux, with_ctx)
        hg_aux = (lower_bounds[:, li].reshape(2, 1, HG_HEADS * HG_DK),)
        o_d, oc_d = _recurrent_branch("hgrn", p, pc, hg_aux, with_ctx)

        o_b = _na_attention(p, pc, _na_bias_table(na_rpb[li]))
        qr, kr = _rope(p, rope_tabs)
        o_c = _wa_attention(qr, kr, p, pc, _sink_rows(wa_sink[li], WA_BLOCK))

        wbr = w_branch[li].at[2].set(w_branch[li][2][_WA_PERM]).astype(BF16)
        wout = w_out[li].astype(BF16)
        lng0, lnb0 = ln_g[li, 0][None], ln_b[li, 0][None]
        x = _merge_out_ln((o_a, o_b, o_c, o_d), p, x, mod, wbr, wout, lng0, lnb0)

        wg = _pad_cols(moe_w_group[li], LANES)
        bg = _pad_cols(moe_b_group[li][None], LANES)
        we = _pad_cols(moe_w_expert[li], LANES)
        be = _pad_cols(moe_b_expert[li][None], LANES)
        wgu = jnp.concatenate([moe_w_gate[li], moe_w_up[li]], -1).reshape(N_EXPERTS, d, 2 * EXPERT_HIDDEN).astype(BF16)
        wd = moe_w_down[li].reshape(N_EXPERTS, EXPERT_HIDDEN, d).astype(BF16)
        lng1, lnb1 = ln_g[li, 1][None], ln_b[li, 1][None]
        moe = functools.partial(_moe_ln, wg=wg, bg=bg, we=we, be=be, wgu=wgu, wd=wd, lng=lng1, lnb=lnb1)
        if with_ctx:
            oc_b = _ctx_attention(pc, "b_q", "b_k", "b_v", True, None)
            oc_c = _ctx_attention(pc, "c_q", "c_k", "c_v", False,
                                  jnp.broadcast_to(wa_sink[li].reshape(2, 4).T[:, :, None, None], (4, 2, lc, LANES)))
            cx = _merge_out_ln((oc_a, oc_b, oc_c, oc_d), pc, cx, modc, wbr, wout, lng0, lnb0)
            cx = moe(cx, modc)
        x = moe(x, mod)
    return x
```

```python
import functools

import numpy as np
import jax
import jax.numpy as jnp
from jax import lax
from jax.experimental import pallas as pl
from jax.experimental.pallas import tpu as pltpu

F32 = jnp.float32
BF16 = jnp.bfloat16
HIGHEST = lax.Precision.HIGHEST

LANES = 128
VMEM_LIMIT = 56 * 1024 * 1024

D_MODEL = 1024
DEPTH = 2
GRID_W = 64
GLA_HEADS, GLA_DK, GLA_DV, GLA_RANK, GLA_TAU = 4, 64, 128, 16, 16.0
NA_HEADS, NA_DH, NA_WIN_R, NA_WIN_C = 8, 64, 8, 16
WA_HEADS, WA_KV_HEADS, WA_DH, WA_RADIUS, WA_BLOCK = 8, 2, 64, 128, 128
ROPE_BASE = 10000.0
HG_HEADS, HG_DK, HG_DV = 4, 128, 128
N_BRANCH, BRANCH_W = 4, 512
N_GROUPS, EXPERTS_PER_GROUP, EXPERT_HIDDEN = 4, 8, 256
N_EXPERTS = N_GROUPS * EXPERTS_PER_GROUP
LN_EPS = 1e-5
ALPHA = (2.0 * DEPTH) ** 0.25

_IN_COLS = (
    ("a_q", 256), ("a_k", 256), ("a_v", 512), ("a_g", 512), ("a_lr", 32),
    ("b_q", 512), ("b_k", 512), ("b_v", 512),
    ("c_q", 512), ("c_k", 128), ("c_v", 128),
    ("d_q", 512), ("d_f", 1024), ("d_i", 512), ("d_g", 512), ("gate", 4096),
)
_ORIG_OFF = {}
_o = 0
for _n, _s in _IN_COLS:
    _ORIG_OFF[_n] = _o
    _o += _s
D_IN = _o
_NEW_ORDER = ("gate", "a_q", "a_k", "a_v", "a_g", "b_q", "b_k", "b_v", "c_q",
              "d_q", "d_f", "d_i", "d_g", "c_k", "c_v", "a_lr")
_SIZES = dict(_IN_COLS)
OFF = {}
_o = 0
for _n in _NEW_ORDER:
    OFF[_n] = _o
    _o += _SIZES[_n]
PROJ_TN = 1536
NP = ((_o + PROJ_TN - 1) // PROJ_TN) * PROJ_TN

SCAN_C = 128
SCAN_LEVELS = (64, 32, 16, 8, 4, 2, 1)

NA_RB = 4
NA_KR = NA_RB + NA_WIN_R - 1


def _cparams(sem, vmem=VMEM_LIMIT):
    return pltpu.CompilerParams(dimension_semantics=sem, vmem_limit_bytes=vmem)


def _dot(a, b, **kw):
    return jnp.dot(a, b, preferred_element_type=F32, **kw)


def _dot_nt(a, b):
    return lax.dot_general(a, b, (((1,), (1,)), ((), ())), preferred_element_type=F32)


def _dot_tn(a, b):
    return lax.dot_general(a, b, (((0,), (0,)), ((), ())), preferred_element_type=F32)


def _sigmoid(x):
    return 1.0 / (1.0 + jnp.exp(-x))


def _silu(x):
    return x * _sigmoid(x)


def _log_sigmoid(x):
    return jnp.minimum(x, 0.0) - jnp.log(1.0 + jnp.exp(-jnp.abs(x)))


def _layer_norm(z, g, b):
    mu = jnp.mean(z, axis=-1, keepdims=True)
    zc = z - mu
    var = jnp.mean(zc * zc, axis=-1, keepdims=True)
    return zc * lax.rsqrt(var + LN_EPS) * g + b


def _mod_kernel(c_ref, w_ref, b_ref, o_ref):
    c = c_ref[...]
    o_ref[0] = _dot(_silu(c), w_ref[0], precision=HIGHEST) + b_ref[0]


def _modulation(cc, w_mod, b_mod):
    depth, d, n = w_mod.shape
    tn = 1536
    return pl.pallas_call(
        _mod_kernel,
        grid=(depth, n // tn),
        in_specs=[pl.BlockSpec((8, d), lambda l, j: (0, 0)),
                  pl.BlockSpec((1, d, tn), lambda l, j: (l, 0, j)),
                  pl.BlockSpec((1, 1, tn), lambda l, j: (l, 0, j))],
        out_specs=pl.BlockSpec((1, 8, tn), lambda l, j: (l, 0, j)),
        out_shape=jax.ShapeDtypeStruct((depth, 8, n), F32),
        compiler_params=_cparams(("parallel", "parallel")),
        name="modulation",
    )(cc, w_mod, b_mod.reshape(depth, 1, n))


def _in_proj_kernel(x_ref, mod_ref, w_ref, o_ref, h_ref):
    @pl.when(pl.program_id(2) == 0)
    def _():
        m = mod_ref[0]
        h_ref[...] = (x_ref[0] * (1.0 + m[1:2]) + m[0:1]).astype(BF16)

    o_ref[0] = _dot(h_ref[...], w_ref[...])


def _in_proj(x, mod, w):
    b, l, d = x.shape
    tm = min(l, 1024)
    return pl.pallas_call(
        _in_proj_kernel,
        grid=(b, l // tm, NP // PROJ_TN),
        in_specs=[pl.BlockSpec((1, tm, d), lambda bi, i, j: (bi, i, 0)),
                  pl.BlockSpec((1, 8, d), lambda bi, i, j: (bi, 0, 0)),
                  pl.BlockSpec((d, PROJ_TN), lambda bi, i, j: (0, j))],
        out_specs=pl.BlockSpec((1, tm, PROJ_TN), lambda bi, i, j: (bi, i, j)),
        out_shape=jax.ShapeDtypeStruct((b, l, NP), F32),
        scratch_shapes=[pltpu.VMEM((tm, d), BF16)],
        compiler_params=_cparams(("parallel", "parallel", "arbitrary")),
        name="in_proj",
    )(x, mod, w)


def _scan_consts(nh):
    c = SCAN_C
    i = np.arange(c)[:, None]
    j = np.arange(c)[None, :]
    tris, sels, msks = [], [], []
    for reverse in (False, True):
        tris.append((j >= i) if reverse else (j <= i))
        sel, masks = [], []
        for h in SCAN_LEVELS:
            same = (i // (2 * h)) == (j // (2 * h))
            base = (np.arange(c) // (2 * h)) * (2 * h)
            if reverse:
                pos = base + h
                m = same & ((i % (2 * h)) < h) & ((j % (2 * h)) >= h)
            else:
                pos = base + h - 1
                m = same & ((i % (2 * h)) >= h) & ((j % (2 * h)) < h)
            sel.append(j == pos[:, None])
            masks.append(m)
        masks.append(i == j)
        sels.append(np.concatenate(sel, 0))
        msks.append(np.stack([np.tile(m, (nh, 1)) for m in masks]))
    return (jnp.asarray(np.stack(tris), BF16), jnp.asarray(np.stack(sels), BF16),
            jnp.asarray(np.stack(msks), F32))


def _split3(x):
    a = x.astype(BF16)
    r = x - a.astype(F32)
    b = r.astype(BF16)
    c = (r - b.astype(F32)).astype(BF16)
    return jnp.concatenate([a, b, c], 1)


def _scan_kernel(*refs, mode, t):
    nh = 2 if mode == "gla" else 1
    it = iter(refs)
    q_refs, k_refs, v_refs = (next(it), next(it)), (next(it), next(it)), (next(it), next(it))
    if mode == "gla":
        lr_refs = (next(it), next(it))
        wlr_ref, blr_ref = next(it), next(it)
    else:
        lb_ref = next(it)
    init_ref, tri_ref, sel_ref, msk_ref = next(it), next(it), next(it), next(it)
    o_refs = (next(it), next(it))
    fin_ref = next(it)
    st_refs = (next(it), next(it))

    @pl.when(pl.program_id(2) == 0)
    def _():
        for z in range(2):
            st_refs[z][...] = init_ref[0, 0, z]

    c = SCAN_C
    nc = t // c
    nlev = len(SCAN_LEVELS)
    row = lax.broadcasted_iota(jnp.int32, (c, LANES), 0)
    lane = lax.broadcasted_iota(jnp.int32, (c, LANES), 1)

    def stack(x):
        if nh == 1:
            return x.astype(BF16)
        return jnp.concatenate([jnp.where(lane < 64, x, 0.0), jnp.where(lane >= 64, x, 0.0)], 0).astype(BF16)

    order = [[ci if z == 0 else nc - 1 - ci for ci in range(nc)] for z in range(2)]
    items = []
    for z in range(2):
        for cc in order[z]:
            sl = pl.ds(cc * c, c)
            if mode == "gla":
                qq = q_refs[z][0, sl, :] * (GLA_DK ** -0.5)
                kk = k_refs[z][0, sl, :]
                logit = _dot(lr_refs[z][0, sl, :].astype(BF16), wlr_ref[z, 0]) + blr_ref[z, 0]
                g = _log_sigmoid(logit) * (1.0 / GLA_TAU)
            else:
                qq = _silu(q_refs[z][0, sl, :])
                lb = lb_ref[z]
                f = k_refs[z][0, sl, :]
                ef = jnp.exp(-jnp.abs(f))
                rf = 1.0 / (1.0 + ef)
                ls = jnp.log1p(-lb) + jnp.minimum(f, 0.0) - jnp.log(1.0 + ef)
                llb = jnp.log(lb)
                g = jnp.maximum(llb, ls) + jnp.log(1.0 + jnp.exp(-jnp.abs(llb - ls)))
                kk = (1.0 - lb) * jnp.where(f >= 0.0, ef * rf, rf)
            items.append(dict(z=z, sl=sl, qq=qq, kk=kk, g=g))

    for z in range(2):
        mine = [it_ for it_ in items if it_["z"] == z]
        b3 = _dot(tri_ref[z], jnp.concatenate([_split3(it_["g"]) for it_ in mine], 1))
        for n, it_ in enumerate(mine):
            o3 = 3 * LANES * n
            it_["bc"] = b3[:, o3:o3 + LANES] + b3[:, o3 + LANES:o3 + 2 * LANES] + b3[:, o3 + 2 * LANES:o3 + 3 * LANES]
        bh = _dot(sel_ref[z], jnp.concatenate([it_["bc"] for it_ in mine], 1).astype(BF16))
        for n, it_ in enumerate(mine):
            it_["bh"] = bh[:, n * LANES:(n + 1) * LANES]

    for it_ in items:
        it_["a"] = jnp.where(msk_ref[it_["z"], nlev] > 0.0, _dot_nt(stack(it_["qq"]), it_["kk"].astype(BF16)), 0.0)
    for li, h in enumerate(SCAN_LEVELS):
        upper = (row & (2 * h - 1)) >= h
        for it_ in items:
            z = it_["z"]
            qrow = jnp.logical_not(upper) if z == 1 else upper
            d = it_["bc"] - it_["bh"][li * c:(li + 1) * c]
            e = jnp.exp(jnp.where(qrow, d, -d))
            zz = jnp.where(qrow, it_["qq"], it_["kk"]) * e
            it_["a"] = jnp.where(msk_ref[z, li] > 0.0, _dot_nt(stack(zz), zz.astype(BF16)), it_["a"])

    for it_ in items:
        z = it_["z"]
        ab = it_["a"].astype(BF16)
        vb = v_refs[z][0, it_["sl"], :].astype(BF16)
        it_["o"] = jnp.concatenate(
            [_dot(ab[hd * c:(hd + 1) * c], vb[:, hd * LANES:(hd + 1) * LANES]) for hd in range(nh)], 1)
        bc = it_["bc"]
        blast = bc[0:1] if z == 1 else bc[c - 1:c]
        it_["qin"] = (it_["qq"] * jnp.exp(bc)).astype(BF16)
        upd = _dot_tn(vb, (it_["kk"] * jnp.exp(blast - bc)).astype(BF16))
        if nh == 2:
            r2 = lax.broadcasted_iota(jnp.int32, (2 * LANES, LANES), 0)
            l2 = lax.broadcasted_iota(jnp.int32, (2 * LANES, LANES), 1)
            upd = jnp.where((r2 < LANES) == (l2 < 64), upd, 0.0)
        it_["upd"] = upd
        it_["dec"] = jnp.exp(blast)

    for z in range(2):
        st = st_refs[z][...]
        for it_ in items:
            if it_["z"] != z:
                continue
            o_refs[z][0, it_["sl"], :] = it_["o"] + _dot_nt(it_["qin"], st.astype(BF16))
            st = st * it_["dec"] + it_["upd"]
        st_refs[z][...] = st

    @pl.when(pl.program_id(2) == pl.num_programs(2) - 1)
    def _():
        for z in range(2):
            fin_ref[0, 0, z] = st_refs[z][...]


def _scan_call(mode, p, init, aux):
    b, l, _ = p.shape
    nh = 2 if mode == "gla" else 1
    ninst = 2 if mode == "gla" else HG_HEADS
    t = min(l, 512)
    nb = l // t
    w = LANES * nh

    def col(name, width, z, extra=0):
        base = (OFF[name] + extra) // width
        if z == 0:
            return pl.BlockSpec((1, t, width), lambda bi, n, i: (bi, i, base + n))
        return pl.BlockSpec((1, t, width), lambda bi, n, i: (bi, nb - 1 - i, base + n))

    def fixed(name, z):
        blk = OFF[name] // LANES
        if z == 0:
            return pl.BlockSpec((1, t, LANES), lambda bi, n, i: (bi, i, blk))
        return pl.BlockSpec((1, t, LANES), lambda bi, n, i: (bi, nb - 1 - i, blk))

    if mode == "gla":
        in_specs = [col("a_q", LANES, 0), col("a_q", LANES, 1), col("a_k", LANES, 0), col("a_k", LANES, 1),
                    col("a_v", w, 0), col("a_v", w, 1), fixed("a_lr", 0), fixed("a_lr", 1),
                    pl.BlockSpec((2, 1, LANES, LANES), lambda bi, n, i: (0, n, 0, 0)),
                    pl.BlockSpec((2, 1, 1, LANES), lambda bi, n, i: (0, n, 0, 0))]
        args = [p] * 8 + [aux[0], aux[1]]
    else:
        fw = HG_HEADS * HG_DK
        in_specs = [col("d_q", LANES, 0), col("d_q", LANES, 1), col("d_f", LANES, 0), col("d_f", LANES, 1, fw),
                    col("d_i", LANES, 0), col("d_i", LANES, 1),
                    pl.BlockSpec((2, 1, LANES), lambda bi, n, i: (0, 0, n))]
        args = [p] * 6 + [aux[0]]
    tri, sel, msk = _scan_consts(nh)
    in_specs += [pl.BlockSpec((1, 1, 2, w, LANES), lambda bi, n, i: (bi, n, 0, 0, 0)),
                 pl.BlockSpec(tri.shape, lambda bi, n, i: (0, 0, 0)),
                 pl.BlockSpec(sel.shape, lambda bi, n, i: (0, 0, 0)),
                 pl.BlockSpec(msk.shape, lambda bi, n, i: (0, 0, 0, 0))]
    args += [init, tri, sel, msk]
    return pl.pallas_call(
        functools.partial(_scan_kernel, mode=mode, t=t),
        grid=(b, ninst, nb),
        in_specs=in_specs,
        out_specs=[pl.BlockSpec((1, t, w), lambda bi, n, i: (bi, i, n)),
                   pl.BlockSpec((1, t, w), lambda bi, n, i: (bi, nb - 1 - i, n)),
                   pl.BlockSpec((1, 1, 2, w, LANES), lambda bi, n, i: (bi, n, 0, 0, 0))],
        out_shape=[jax.ShapeDtypeStruct((b, l, ninst * w), F32),
                   jax.ShapeDtypeStruct((b, l, ninst * w), F32),
                   jax.ShapeDtypeStruct((b, ninst, 2, w, LANES), F32)],
        scratch_shapes=[pltpu.VMEM((w, LANES), F32), pltpu.VMEM((w, LANES), F32)],
        compiler_params=_cparams(("parallel", "parallel", "arbitrary")),
        name=f"scan_{mode}",
    )(*args)


def _recurrent_branch(mode, p, pc, aux):
    b = p.shape[0]
    nh = 2 if mode == "gla" else 1
    ninst = 2 if mode == "gla" else HG_HEADS
    zero = jnp.zeros((b, ninst, 2, LANES * nh, LANES), F32)
    oc_f, oc_b, sc = _scan_call(mode, pc, zero, aux)
    o_f, o_b, _ = _scan_call(mode, p, sc, aux)
    return (o_f, o_b), (oc_f, oc_b)


def _na_bias_table(rpb):
    cols = np.arange(GRID_W)
    cs = np.clip(cols - NA_WIN_C // 2, 0, GRID_W - NA_WIN_C)
    kc = np.arange(GRID_W)
    valid = (kc[None, :] >= cs[:, None]) & (kc[None, :] < cs[:, None] + NA_WIN_C)
    col_idx = np.clip(kc[None, :] - cols[:, None] + NA_WIN_C - 1, 0, 2 * NA_WIN_C - 2)
    tiles = jnp.where(valid[None, None], rpb[:, :, col_idx], -jnp.inf)
    nrow = 2 * NA_WIN_R - 1
    tiles = jnp.concatenate([tiles, jnp.full((NA_HEADS, 1, GRID_W, GRID_W), -jnp.inf, F32)], 1)
    rl = np.arange(NA_RB)[:, None]
    kl = np.arange(NA_KR)[None, :]
    spec = ((np.zeros_like(rl), kl - rl + NA_WIN_R - 1),
            (rl, kl - rl + NA_WIN_R // 2 - 1),
            (np.full_like(rl, NA_RB - 1), kl - rl))
    tidx = np.stack([np.where((kl >= lo) & (kl < lo + NA_WIN_R), off, nrow) for lo, off in spec])
    assert tidx.min() >= 0 and tidx.max() <= nrow
    t = tiles[:, tidx]
    t = jnp.transpose(t, (0, 1, 2, 4, 3, 5))
    return t.reshape(NA_HEADS, 3, NA_RB * GRID_W, NA_KR * GRID_W).astype(F32)


def _softmax_av(parts, extra=None):
    m = functools.reduce(jnp.maximum, [jnp.max(s, axis=-1, keepdims=True) for s, _ in parts])
    if extra is not None:
        m = jnp.maximum(m, extra)
    den = 0.0
    acc = 0.0
    for s, v in parts:
        e = jnp.exp(s - m)
        den = den + jnp.sum(e, axis=-1, keepdims=True)
        acc = acc + _dot(e.astype(BF16), v)
    if extra is not None:
        den = den + jnp.exp(extra - m)
    return acc / den


def _na_kernel(q_ref, k_ref, v_ref, kc_ref, vc_ref, bias_ref, o_ref, *, rows):
    kc = kc_ref[0].astype(BF16)
    vc = vc_ref[0].astype(BF16)
    qn, kn = NA_RB * GRID_W, NA_KR * GRID_W
    nblk = rows // NA_RB
    lane = lax.broadcasted_iota(jnp.int32, (qn, LANES), 1)

    def body(bi, carry):
        r0 = bi * NA_RB
        w0 = jnp.clip(r0 - NA_WIN_R // 2, 0, rows - NA_KR)
        cls = jnp.where(bi == 0, 0, jnp.where(bi == nblk - 1, 2, 1))
        qs = pl.ds(pl.multiple_of(r0 * GRID_W, qn), qn)
        ks = pl.ds(pl.multiple_of(w0 * GRID_W, GRID_W), kn)
        q = q_ref[0, qs, :] * (NA_DH ** -0.5)
        kw = k_ref[0, ks, :].astype(BF16)
        vw = v_ref[0, ks, :].astype(BF16)
        outs = []
        for hd in range(2):
            qm = jnp.where((lane < 64) if hd == 0 else (lane >= 64), q, 0.0).astype(BF16)
            s_loc = _dot_nt(qm, kw) + bias_ref[hd, cls]
            s_ctx = _dot_nt(qm, kc)
            outs.append(_softmax_av([(s_loc, vw), (s_ctx, vc)]))
        o_ref[0, qs, :] = jnp.where(lane < 64, outs[0], outs[1])
        return carry

    lax.fori_loop(0, nblk, body, 0)


def _na_attention(p, pc, bias):
    b, l, _ = p.shape
    lc = pc.shape[1]
    rows = l // GRID_W
    assert rows % NA_RB == 0 and rows >= NA_KR + 1
    npair = NA_HEADS // 2

    def col(name):
        base = OFF[name] // LANES
        return pl.BlockSpec((1, l, LANES), lambda bi, n: (bi, 0, base + n))

    def ccol(name):
        base = OFF[name] // LANES
        return pl.BlockSpec((1, lc, LANES), lambda bi, n: (bi, 0, base + n))

    return pl.pallas_call(
        functools.partial(_na_kernel, rows=rows),
        grid=(b, npair),
        in_specs=[col("b_q"), col("b_k"), col("b_v"), ccol("b_k"), ccol("b_v"),
                  pl.BlockSpec((2, 3, NA_RB * GRID_W, NA_KR * GRID_W), lambda bi, n: (n, 0, 0, 0))],
        out_specs=pl.BlockSpec((1, l, LANES), lambda bi, n: (bi, 0, n)),
        out_shape=jax.ShapeDtypeStruct((b, l, NA_HEADS * NA_DH), F32),
        compiler_params=_cparams(("parallel", "parallel")),
        name="na_attention",
    )(p, p, p, pc, pc, bias)


def _ctx_attn_kernel(q_ref, k_ref, v_ref, sink_ref, o_ref, *, use_sink):
    lc = q_ref.shape[1]
    lane = lax.broadcasted_iota(jnp.int32, (lc, LANES), 1)
    q = q_ref[0] * (NA_DH ** -0.5)
    kb = k_ref[0].astype(BF16)
    vb = v_ref[0].astype(BF16)
    outs = []
    for hd in range(2):
        qm = jnp.where((lane < 64) if hd == 0 else (lane >= 64), q, 0.0).astype(BF16)
        s = _dot_nt(qm, kb)
        extra = sink_ref[0, hd][:, 0:1] if use_sink else None
        outs.append(_softmax_av([(s, vb)], extra))
    o_ref[0] = jnp.where(lane < 64, outs[0], outs[1])


def _ctx_attention(pc, qname, kname, vname, kv_per_q, sink_tab):
    b, lc, _ = pc.shape
    nblk = 4
    use_sink = sink_tab is not None
    if sink_tab is None:
        sink_tab = jnp.zeros((nblk, 2, lc, LANES), F32)
    qb, kb, vb = OFF[qname] // LANES, OFF[kname] // LANES, OFF[vname] // LANES
    kmul = 1 if kv_per_q else 0
    return pl.pallas_call(
        functools.partial(_ctx_attn_kernel, use_sink=use_sink),
        grid=(b, nblk),
        in_specs=[pl.BlockSpec((1, lc, LANES), lambda bi, n: (bi, 0, qb + n)),
                  pl.BlockSpec((1, lc, LANES), lambda bi, n: (bi, 0, kb + kmul * n)),
                  pl.BlockSpec((1, lc, LANES), lambda bi, n: (bi, 0, vb + kmul * n)),
                  pl.BlockSpec((1, 2, lc, LANES), lambda bi, n: (n, 0, 0, 0))],
        out_specs=pl.BlockSpec((1, lc, LANES), lambda bi, n: (bi, 0, n)),
        out_shape=jax.ShapeDtypeStruct((b, lc, nblk * LANES), F32),
        compiler_params=_cparams(("parallel", "parallel")),
        name="ctx_attention",
    )(pc, pc, pc, sink_tab)


def _rope_tables(l):
    nf = WA_DH // 4
    pos = np.arange(l)
    inv = ROPE_BASE ** (-np.arange(nf, dtype=np.float32) / nf)
    d = np.arange(WA_DH)
    p = np.where(d[None, :] < WA_DH // 2, (pos // GRID_W)[:, None], (pos % GRID_W)[:, None]).astype(np.float32)
    ang = p * inv[d % nf][None, :].astype(np.float32)
    cos, sin = np.cos(ang), np.sin(ang)
    first = (d % (2 * nf)) < nf
    ta = np.where(first[None, :], -sin, 0.0)
    tb = np.where(first[None, :], 0.0, sin)
    tile = lambda t: jnp.asarray(np.tile(t, (1, LANES // WA_DH)), F32)
    return tile(cos), tile(ta), tile(tb)


def _rope_kernel(q_ref, k_ref, c_ref, a_ref, b_ref, qo_ref, ko_ref):
    c, a, b = c_ref[...], a_ref[...], b_ref[...]
    nf = WA_DH // 4

    def rot(u):
        return u * c + pltpu.roll(u, LANES - nf, 1) * a + pltpu.roll(u, nf, 1) * b

    for j in range(q_ref.shape[2] // LANES):
        qo_ref[0, :, j * LANES:(j + 1) * LANES] = rot(q_ref[0, :, j * LANES:(j + 1) * LANES])
    ko_ref[0] = rot(k_ref[0])


def _rope(p, tabs):
    b, l, _ = p.shape
    t = min(l, 512)
    qw = WA_HEADS * WA_DH
    tspec = pl.BlockSpec((t, LANES), lambda bi, i: (i, 0))
    return pl.pallas_call(
        _rope_kernel,
        grid=(b, l // t),
        in_specs=[pl.BlockSpec((1, t, qw), lambda bi, i: (bi, i, OFF["c_q"] // qw)),
                  pl.BlockSpec((1, t, LANES), lambda bi, i: (bi, i, OFF["c_k"] // LANES)),
                  tspec, tspec, tspec],
        out_specs=[pl.BlockSpec((1, t, qw), lambda bi, i: (bi, i, 0)),
                   pl.BlockSpec((1, t, LANES), lambda bi, i: (bi, i, 0))],
        out_shape=[jax.ShapeDtypeStruct((b, l, qw), F32), jax.ShapeDtypeStruct((b, l, LANES), F32)],
        compiler_params=_cparams(("parallel", "parallel")),
        name="rope",
    )(p, p, *tabs)


def _wa_kernel(q_ref, k_ref, v_ref, kc_ref, vc_ref, sink_ref, o_ref, *, l):
    n = pl.program_id(1)
    blk = WA_BLOCK
    span = 3 * blk
    start = jnp.clip((n - 1) * blk, 0, l - span)
    ks = pl.ds(pl.multiple_of(start, blk), span)
    kw = k_ref[0, ks, :].astype(BF16)
    vw = v_ref[0, ks, :].astype(BF16)
    kc = kc_ref[0].astype(BF16)
    vc = vc_ref[0].astype(BF16)
    ngrp = WA_HEADS // 2
    lane = lax.broadcasted_iota(jnp.int32, (blk, LANES), 1)
    ii = lax.broadcasted_iota(jnp.int32, (ngrp * blk, span), 0) & (blk - 1)
    jj = lax.broadcasted_iota(jnp.int32, (ngrp * blk, span), 1)
    dist = (jj + start) - (ii + n * blk)
    valid = jnp.abs(dist) <= WA_RADIUS
    halves = []
    for half in range(2):
        keep = (lane < 64) if half == 0 else (lane >= 64)
        qs = jnp.concatenate(
            [jnp.where(keep, q_ref[0, :, j * LANES:(j + 1) * LANES] * (WA_DH ** -0.5), 0.0) for j in range(ngrp)],
            0).astype(BF16)
        s_loc = jnp.where(valid, _dot_nt(qs, kw), -jnp.inf)
        s_ctx = _dot_nt(qs, kc)
        halves.append(_softmax_av([(s_loc, vw), (s_ctx, vc)], sink_ref[half][:, 0:1]))
    for j in range(ngrp):
        o_ref[0, :, j * LANES:(j + 1) * LANES] = jnp.where(lane < 64, halves[0][j * blk:(j + 1) * blk],
                                                           halves[1][j * blk:(j + 1) * blk])


def _wa_attention(qr, kr, p, pc, sink_rows):
    b, l, qw = qr.shape
    lc = pc.shape[1]
    return pl.pallas_call(
        functools.partial(_wa_kernel, l=l),
        grid=(b, l // WA_BLOCK),
        in_specs=[pl.BlockSpec((1, WA_BLOCK, qw), lambda bi, n: (bi, n, 0)),
                  pl.BlockSpec((1, l, LANES), lambda bi, n: (bi, 0, 0)),
                  pl.BlockSpec((1, l, LANES), lambda bi, n: (bi, 0, OFF["c_v"] // LANES)),
                  pl.BlockSpec((1, lc, LANES), lambda bi, n: (bi, 0, OFF["c_k"] // LANES)),
                  pl.BlockSpec((1, lc, LANES), lambda bi, n: (bi, 0, OFF["c_v"] // LANES)),
                  pl.BlockSpec(sink_rows.shape, lambda bi, n: (0, 0, 0))],
        out_specs=pl.BlockSpec((1, WA_BLOCK, qw), lambda bi, n: (bi, n, 0)),
        out_shape=jax.ShapeDtypeStruct((b, l, qw), F32),
        compiler_params=_cparams(("parallel", "arbitrary")),
        name="wa_attention",
    )(qr, kr, p, pc, pc, sink_rows)


def _rms_gate(o, og):
    outs = []
    for hd in range(o.shape[1] // LANES):
        oh = o[:, hd * LANES:(hd + 1) * LANES]
        r = lax.rsqrt(jnp.mean(oh * oh, axis=-1, keepdims=True) + 1e-6)
        outs.append(oh * r * _silu(og[:, hd * LANES:(hd + 1) * LANES]))
    return jnp.concatenate(outs, 1)


def _merge_kernel(oaf_ref, oab_ref, ag_ref, ob_ref, oc_ref, odf_ref, odb_ref, dg_ref, gate_ref, x_ref, mod_ref,
                  wbr_ref, wout_ref, lng_ref, lnb_ref, o_ref):
    branches = (_rms_gate(oaf_ref[0] + oab_ref[0], ag_ref[0]), ob_ref[0], oc_ref[0],
                _rms_gate(odf_ref[0] + odb_ref[0], dg_ref[0]))
    acc = 0.0
    for i, br in enumerate(branches):
        y = _dot(br.astype(BF16), wbr_ref[i])
        acc = acc + _sigmoid(gate_ref[0, :, i * D_MODEL:(i + 1) * D_MODEL]) * y
    mix = _dot(acc.astype(BF16), wout_ref[...])
    m = mod_ref[0]
    z = ALPHA * x_ref[0] + m[2:3] * mix
    o_ref[0] = _layer_norm(z, lng_ref[...], lnb_ref[...])


def _merge_out_ln(o_a, o_b, o_c, o_d, p, x, mod, wbr, wout, lng, lnb):
    b, l, d = x.shape
    tm = min(l, 256)
    gw = N_BRANCH * D_MODEL
    bspec = pl.BlockSpec((1, tm, BRANCH_W), lambda bi, i: (bi, i, 0))

    def pcol(name):
        blk = OFF[name] // BRANCH_W
        return pl.BlockSpec((1, tm, BRANCH_W), lambda bi, i: (bi, i, blk))

    return pl.pallas_call(
        _merge_kernel,
        grid=(b, l // tm),
        in_specs=[bspec, bspec, pcol("a_g"), bspec, bspec, bspec, bspec, pcol("d_g"),
                  pl.BlockSpec((1, tm, gw), lambda bi, i: (bi, i, OFF["gate"] // gw)),
                  pl.BlockSpec((1, tm, d), lambda bi, i: (bi, i, 0)),
                  pl.BlockSpec((1, 8, d), lambda bi, i: (bi, 0, 0)),
                  pl.BlockSpec((N_BRANCH, BRANCH_W, d), lambda bi, i: (0, 0, 0)),
                  pl.BlockSpec((d, d), lambda bi, i: (0, 0)),
                  pl.BlockSpec((1, d), lambda bi, i: (0, 0)),
                  pl.BlockSpec((1, d), lambda bi, i: (0, 0))],
        out_specs=pl.BlockSpec((1, tm, d), lambda bi, i: (bi, i, 0)),
        out_shape=jax.ShapeDtypeStruct((b, l, d), F32),
        compiler_params=_cparams(("parallel", "parallel")),
        name="merge_out_ln",
    )(o_a[0], o_a[1], p, o_b, o_c, o_d[0], o_d[1], p, p, x, mod, wbr, wout, lng, lnb)


def _route(h, wg_ref, bg_ref, we_ref, be_ref):
    tm = h.shape[0]
    lane = lax.broadcasted_iota(jnp.int32, (tm, LANES), 1)
    ninf = -jnp.inf
    gl = jnp.where(lane < N_GROUPS, _dot(h, wg_ref[...], precision=HIGHEST) + bg_ref[...], ninf)
    gmax = jnp.max(gl, axis=-1, keepdims=True)
    gidx = jnp.min(jnp.where(gl == gmax, lane, LANES), axis=-1, keepdims=True)
    gw = 1.0 / jnp.sum(jnp.exp(gl - gmax), axis=-1, keepdims=True)
    el = _dot(h, we_ref[...], precision=HIGHEST) + be_ref[...]
    ingrp = jnp.logical_and(lane < N_EXPERTS, lax.shift_right_logical(lane, 3) == gidx)
    e1 = jnp.where(ingrp, el, ninf)
    m1 = jnp.max(e1, axis=-1, keepdims=True)
    i1 = jnp.min(jnp.where(e1 == m1, lane, LANES), axis=-1, keepdims=True)
    e2 = jnp.where(lane == i1, ninf, e1)
    m2 = jnp.max(e2, axis=-1, keepdims=True)
    i2 = jnp.min(jnp.where(e2 == m2, lane, LANES), axis=-1, keepdims=True)
    t = jnp.exp(m2 - m1)
    w1 = 1.0 / (1.0 + t)
    w2 = t / (1.0 + t)
    return gw * (jnp.where(lane == i1, w1, 0.0) + jnp.where(lane == i2, w2, 0.0))


def _moe_kernel(x_ref, mod_ref, wg_ref, bg_ref, we_ref, be_ref, wgu_ref, wd_ref, lng_ref, lnb_ref, o_ref,
                h_ref, wts_ref, acc_ref):
    e = pl.program_id(2)
    m = mod_ref[0]

    @pl.when(e == 0)
    def _():
        h = x_ref[0] * (1.0 + m[4:5]) + m[3:4]
        h_ref[...] = h.astype(BF16)
        wts_ref[...] = _route(h, wg_ref, bg_ref, we_ref, be_ref)
        acc_ref[...] = jnp.zeros_like(acc_ref)

    lane = lax.broadcasted_iota(jnp.int32, wts_ref.shape, 1)
    wcol = jnp.sum(jnp.where(lane == e, wts_ref[...], 0.0), axis=-1, keepdims=True)
    gu = _dot(h_ref[...], wgu_ref[0])
    hid = _silu(gu[:, :EXPERT_HIDDEN]) * gu[:, EXPERT_HIDDEN:] * wcol
    acc_ref[...] += _dot(hid.astype(BF16), wd_ref[0])

    @pl.when(e == pl.num_programs(2) - 1)
    def _():
        z = ALPHA * x_ref[0] + m[5:6] * acc_ref[...]
        o_ref[0] = _layer_norm(z, lng_ref[...], lnb_ref[...])


def _moe_ln(x, mod, wg, bg, we, be, wgu, wd, lng, lnb):
    b, l, d = x.shape
    tm = min(l, 1024)
    cst = lambda shape: pl.BlockSpec(shape, lambda bi, i, e: tuple(0 for _ in shape))
    return pl.pallas_call(
        _moe_kernel,
        grid=(b, l // tm, N_EXPERTS),
        in_specs=[pl.BlockSpec((1, tm, d), lambda bi, i, e: (bi, i, 0)),
                  pl.BlockSpec((1, 8, d), lambda bi, i, e: (bi, 0, 0)),
                  cst((d, LANES)), cst((1, LANES)), cst((d, LANES)), cst((1, LANES)),
                  pl.BlockSpec((1, d, 2 * EXPERT_HIDDEN), lambda bi, i, e: (e, 0, 0)),
                  pl.BlockSpec((1, EXPERT_HIDDEN, d), lambda bi, i, e: (e, 0, 0)),
                  cst((1, d)), cst((1, d))],
        out_specs=pl.BlockSpec((1, tm, d), lambda bi, i, e: (bi, i, 0)),
        out_shape=jax.ShapeDtypeStruct((b, l, d), F32),
        scratch_shapes=[pltpu.VMEM((tm, d), BF16), pltpu.VMEM((tm, LANES), F32), pltpu.VMEM((tm, d), F32)],
        compiler_params=_cparams(("parallel", "parallel", "arbitrary")),
        name="moe_ln",
    )(x, mod, wg, bg, we, be, wgu, wd, lng, lnb)


def _pad_cols(w, n):
    return jnp.pad(w, ((0, 0), (0, n - w.shape[1])))


def _wa_head_pairs(w, axis):
    shp = w.shape
    w = w.reshape(shp[:axis] + (2, WA_HEADS // 2, WA_DH) + shp[axis + 1:])
    w = jnp.swapaxes(w, axis, axis + 1)
    return w.reshape(shp)


def _prep_w_in(w):
    segs = []
    for name in _NEW_ORDER:
        s = w[:, _ORIG_OFF[name]:_ORIG_OFF[name] + _SIZES[name]]
        if name == "c_q":
            s = _wa_head_pairs(s, 1)
        segs.append(s)
    return _pad_cols(jnp.concatenate(segs, 1), NP).astype(BF16)


def _prep_gla_gate(lr_w, lr_b):
    w = jnp.zeros((2, 2, LANES, LANES), F32)
    for z in range(2):
        for n in range(2):
            w = w.at[z, n, z * GLA_RANK:(z + 1) * GLA_RANK, :].set(lr_w[z][:, n * LANES:(n + 1) * LANES])
    return w.astype(BF16), lr_b.reshape(2, 2, 1, LANES)


def _sink_rows(sink, rows_per_head):
    s = sink.reshape(2, 4)
    return jnp.broadcast_to(s[:, :, None, None], (2, 4, rows_per_head, LANES)).reshape(2, 4 * rows_per_head, LANES)


def kernel(x, c, ctx, c_ctx, w_mod, b_mod, w_in, gla_lr_w, gla_lr_b, hg_gamma, na_rpb, wa_sink, w_branch, w_out,
           ln_g, ln_b, moe_w_group, moe_b_group, moe_w_expert, moe_b_expert, moe_w_gate, moe_w_up, moe_w_down):
    b, l, d = x.shape
    lc = ctx.shape[1]
    assert d == D_MODEL and b + 1 <= 8 and l % 512 == 0 and l >= 3 * WA_BLOCK and lc % SCAN_C == 0

    cc = jnp.concatenate([c, c_ctx[None], jnp.zeros((8 - b - 1, d), F32)], 0)
    mods = _modulation(cc, w_mod, b_mod)
    cum = jnp.cumsum(jax.nn.softmax(hg_gamma.astype(F32), axis=1), axis=1)
    lower_bounds = cum - cum[:, :1]
    rope_tabs = _rope_tables(l)

    cx = ctx
    for li in range(DEPTH):
        with_ctx = li < DEPTH - 1
        mod6 = mods[li].reshape(8, 6, d)
        mod = jnp.pad(mod6[:b], ((0, 0), (0, 2), (0, 0)))
        modc = jnp.broadcast_to(jnp.pad(mod6[b], ((0, 2), (0, 0)))[None], (b, 8, d))
        w_in_p = _prep_w_in(w_in[li])
        p = _in_proj(x, mod, w_in_p)
        pc = _in_proj(cx, modc, w_in_p)

        gla_aux = _prep_gla_gate(gla_lr_w[li], gla_lr_b[li])
        o_a, oc_a = _recurrent_branch("gla", p, pc, gla_aux)
        hg_aux = (lower_bounds[:, li].reshape(2, 1, HG_HEADS * HG_DK),)
        o_d, oc_d = _recurrent_branch("hgrn", p, pc, hg_aux)

        o_b = _na_attention(p, pc, _na_bias_table(na_rpb[li]))
        qr, kr = _rope(p, rope_tabs)
        o_c = _wa_attention(qr, kr, p, pc, _sink_rows(wa_sink[li], WA_BLOCK))

        wbr = jnp.stack([w_branch[li, 0], w_branch[li, 1], _wa_head_pairs(w_branch[li, 2], 0),
                         w_branch[li, 3]]).astype(BF16)
        wout = w_out[li].astype(BF16)
        lng0, lnb0 = ln_g[li, 0][None], ln_b[li, 0][None]
        x = _merge_out_ln(o_a, o_b, o_c, o_d, p, x, mod, wbr, wout, lng0, lnb0)

        wg = _pad_cols(moe_w_group[li], LANES)
        bg = _pad_cols(moe_b_group[li][None], LANES)
        we = _pad_cols(moe_w_expert[li], LANES)
        be = _pad_cols(moe_b_expert[li][None], LANES)
        wgu = jnp.concatenate([moe_w_gate[li].astype(BF16), moe_w_up[li].astype(BF16)], -1)
        wgu = wgu.reshape(N_EXPERTS, d, 2 * EXPERT_HIDDEN)
        wd = moe_w_down[li].reshape(N_EXPERTS, EXPERT_HIDDEN, d).astype(BF16)
        lng1, lnb1 = ln_g[li, 1][None], ln_b[li, 1][None]
        moe = functools.partial(_moe_ln, wg=wg, bg=bg, we=we, be=be, wgu=wgu, wd=wd, lng=lng1, lnb=lnb1)
        if with_ctx:
            oc_b = _ctx_attention(pc, "b_q", "b_k", "b_v", True, None)
            oc_c = _ctx_attention(pc, "c_q", "c_k", "c_v", False,
                                  jnp.broadcast_to(wa_sink[li].reshape(2, 4).T[:, :, None, None], (4, 2, lc, LANES)))
            cx = _merge_out_ln(oc_a, oc_b, oc_c, oc_d, pc, cx, modc, wbr, wout, lng0, lnb0)
            cx = moe(cx, modc)
        x = moe(x, mod)
    return x
```

```python
import functools

import numpy as np
import jax
import jax.numpy as jnp
from jax import lax
from jax.experimental import pallas as pl
from jax.experimental.pallas import tpu as pltpu

F32 = jnp.float32
BF16 = jnp.bfloat16
HIGHEST = lax.Precision.HIGHEST

LANES = 128
VMEM_LIMIT = 56 * 1024 * 1024

D_MODEL = 1024
DEPTH = 2
GRID_W = 64
GLA_HEADS, GLA_DK, GLA_DV, GLA_RANK, GLA_TAU = 4, 64, 128, 16, 16.0
NA_HEADS, NA_DH, NA_WIN_R, NA_WIN_C = 8, 64, 8, 16
WA_HEADS, WA_KV_HEADS, WA_DH, WA_RADIUS, WA_BLOCK = 8, 2, 64, 128, 128
ROPE_BASE = 10000.0
HG_HEADS, HG_DK, HG_DV = 4, 128, 128
N_BRANCH, BRANCH_W = 4, 512
N_GROUPS, EXPERTS_PER_GROUP, EXPERT_HIDDEN = 4, 8, 256
N_EXPERTS = N_GROUPS * EXPERTS_PER_GROUP
LN_EPS = 1e-5
ALPHA = (2.0 * DEPTH) ** 0.25

_IN_COLS = (
    ("a_q", 256), ("a_k", 256), ("a_v", 512), ("a_g", 512), ("a_lr", 32),
    ("b_q", 512), ("b_k", 512), ("b_v", 512),
    ("c_q", 512), ("c_k", 128), ("c_v", 128),
    ("d_q", 512), ("d_f", 1024), ("d_i", 512), ("d_g", 512), ("gate", 4096),
)
_ORIG_OFF = {}
_o = 0
for _n, _s in _IN_COLS:
    _ORIG_OFF[_n] = _o
    _o += _s
D_IN = _o
_NEW_ORDER = ("gate", "a_q", "a_k", "a_v", "a_g", "b_q", "b_k", "b_v", "c_q",
              "d_q", "d_f", "d_i", "d_g", "c_k", "c_v", "a_lr")
_SIZES = dict(_IN_COLS)
OFF = {}
_o = 0
for _n in _NEW_ORDER:
    OFF[_n] = _o
    _o += _SIZES[_n]
PROJ_TN = 1536
NP = ((_o + PROJ_TN - 1) // PROJ_TN) * PROJ_TN

SCAN_C = 128
SCAN_LEVELS = (64, 32, 16, 8, 4, 2, 1)

MOE_TM = 256

NA_RB = 4
NA_KR = NA_RB + NA_WIN_R - 1


def _cparams(sem, vmem=VMEM_LIMIT):
    return pltpu.CompilerParams(dimension_semantics=sem, vmem_limit_bytes=vmem)


def _dot(a, b, **kw):
    return jnp.dot(a, b, preferred_element_type=F32, **kw)


def _dot_nt(a, b):
    return lax.dot_general(a, b, (((1,), (1,)), ((), ())), preferred_element_type=F32)


def _dot_tn(a, b):
    return lax.dot_general(a, b, (((0,), (0,)), ((), ())), preferred_element_type=F32)


def _sigmoid(x):
    return 1.0 / (1.0 + jnp.exp(-x))


def _silu(x):
    return x * _sigmoid(x)


def _log_sigmoid(x):
    return jnp.minimum(x, 0.0) - jnp.log(1.0 + jnp.exp(-jnp.abs(x)))


def _layer_norm(z, g, b):
    mu = jnp.mean(z, axis=-1, keepdims=True)
    zc = z - mu
    var = jnp.mean(zc * zc, axis=-1, keepdims=True)
    return zc * lax.rsqrt(var + LN_EPS) * g + b


def _mod_kernel(c_ref, w_ref, b_ref, o_ref):
    c = c_ref[...]
    o_ref[0] = _dot(_silu(c), w_ref[0], precision=HIGHEST) + b_ref[0]


def _modulation(cc, w_mod, b_mod):
    depth, d, n = w_mod.shape
    tn = 1536
    return pl.pallas_call(
        _mod_kernel,
        grid=(depth, n // tn),
        in_specs=[pl.BlockSpec((8, d), lambda l, j: (0, 0)),
                  pl.BlockSpec((1, d, tn), lambda l, j: (l, 0, j)),
                  pl.BlockSpec((1, 1, tn), lambda l, j: (l, 0, j))],
        out_specs=pl.BlockSpec((1, 8, tn), lambda l, j: (l, 0, j)),
        out_shape=jax.ShapeDtypeStruct((depth, 8, n), F32),
        compiler_params=_cparams(("parallel", "parallel")),
        name="modulation",
    )(cc, w_mod, b_mod.reshape(depth, 1, n))


def _in_proj_kernel(x_ref, mod_ref, w_ref, o_ref, h_ref):
    @pl.when(pl.program_id(2) == 0)
    def _():
        m = mod_ref[0]
        h_ref[...] = (x_ref[...] * (1.0 + m[1:2]) + m[0:1]).astype(BF16)

    o_ref[0] = _dot(h_ref[...], w_ref[...])


def _in_proj(x, b, l, mod, w):
    d = x.shape[1]
    tm = min(l, 1024)
    nt = l // tm
    return pl.pallas_call(
        _in_proj_kernel,
        grid=(b, nt, NP // PROJ_TN),
        in_specs=[pl.BlockSpec((tm, d), lambda bi, i, j: (bi * nt + i, 0)),
                  pl.BlockSpec((1, 8, d), lambda bi, i, j: (bi, 0, 0)),
                  pl.BlockSpec((d, PROJ_TN), lambda bi, i, j: (0, j))],
        out_specs=pl.BlockSpec((1, tm, PROJ_TN), lambda bi, i, j: (bi, i, j)),
        out_shape=jax.ShapeDtypeStruct((b, l, NP), F32),
        scratch_shapes=[pltpu.VMEM((tm, d), BF16)],
        compiler_params=_cparams(("parallel", "parallel", "arbitrary")),
        name="in_proj",
    )(x, mod, w)


def _scan_consts(nh):
    c = SCAN_C
    i = np.arange(c)[:, None]
    j = np.arange(c)[None, :]
    tris, sels, msks = [], [], []
    for reverse in (False, True):
        tris.append((j >= i) if reverse else (j <= i))
        sel, masks = [], []
        for h in SCAN_LEVELS:
            same = (i // (2 * h)) == (j // (2 * h))
            base = (np.arange(c) // (2 * h)) * (2 * h)
            if reverse:
                pos = base + h
                m = same & ((i % (2 * h)) < h) & ((j % (2 * h)) >= h)
            else:
                pos = base + h - 1
                m = same & ((i % (2 * h)) >= h) & ((j % (2 * h)) < h)
            sel.append(j == pos[:, None])
            masks.append(m)
        masks.append(i == j)
        sels.append(np.concatenate(sel, 0))
        msks.append(np.stack([np.tile(m, (nh, 1)) for m in masks]))
    return (jnp.asarray(np.stack(tris), BF16), jnp.asarray(np.stack(sels), BF16),
            jnp.asarray(np.stack(msks), F32))


def _split3(x):
    a = x.astype(BF16)
    r = x - a.astype(F32)
    b = r.astype(BF16)
    c = (r - b.astype(F32)).astype(BF16)
    return jnp.concatenate([a, b, c], 1)


def _scan_kernel(*refs, mode, t):
    nh = 2 if mode == "gla" else 1
    it = iter(refs)
    q_refs, k_refs, v_refs = (next(it), next(it)), (next(it), next(it)), (next(it), next(it))
    if mode == "gla":
        lr_refs = (next(it), next(it))
        wlr_ref, blr_ref = next(it), next(it)
    else:
        lb_ref = next(it)
    init_ref, tri_ref, sel_ref, msk_ref = next(it), next(it), next(it), next(it)
    o_refs = (next(it), next(it))
    fin_ref = next(it)
    st_refs = (next(it), next(it))

    @pl.when(pl.program_id(2) == 0)
    def _():
        for z in range(2):
            st_refs[z][...] = init_ref[0, 0, z]

    c = SCAN_C
    nc = t // c
    nlev = len(SCAN_LEVELS)
    row = lax.broadcasted_iota(jnp.int32, (c, LANES), 0)
    lane = lax.broadcasted_iota(jnp.int32, (c, LANES), 1)

    def stack(x):
        if nh == 1:
            return x.astype(BF16)
        return jnp.concatenate([jnp.where(lane < 64, x, 0.0), jnp.where(lane >= 64, x, 0.0)], 0).astype(BF16)

    order = [[ci if z == 0 else nc - 1 - ci for ci in range(nc)] for z in range(2)]
    items = []
    for z in range(2):
        for cc in order[z]:
            sl = pl.ds(cc * c, c)
            if mode == "gla":
                qq = q_refs[z][0, sl, :] * (GLA_DK ** -0.5)
                kk = k_refs[z][0, sl, :]
                logit = _dot(lr_refs[z][0, sl, :].astype(BF16), wlr_ref[z, 0]) + blr_ref[z, 0]
                g = _log_sigmoid(logit) * (1.0 / GLA_TAU)
            else:
                qq = _silu(q_refs[z][0, sl, :])
                lb = lb_ref[z]
                f = k_refs[z][0, sl, :]
                ef = jnp.exp(-jnp.abs(f))
                rf = 1.0 / (1.0 + ef)
                ls = jnp.log1p(-lb) + jnp.minimum(f, 0.0) - jnp.log(1.0 + ef)
                llb = jnp.log(lb)
                g = jnp.maximum(llb, ls) + jnp.log(1.0 + jnp.exp(-jnp.abs(llb - ls)))
                kk = (1.0 - lb) * jnp.where(f >= 0.0, ef * rf, rf)
            items.append(dict(z=z, sl=sl, qq=qq, kk=kk, g=g))

    for z in range(2):
        mine = [it_ for it_ in items if it_["z"] == z]
        b3 = _dot(tri_ref[z], jnp.concatenate([_split3(it_["g"]) for it_ in mine], 1))
        for n, it_ in enumerate(mine):
            o3 = 3 * LANES * n
            it_["bc"] = b3[:, o3:o3 + LANES] + b3[:, o3 + LANES:o3 + 2 * LANES] + b3[:, o3 + 2 * LANES:o3 + 3 * LANES]
        bh = _dot(sel_ref[z], jnp.concatenate([it_["bc"] for it_ in mine], 1).astype(BF16))
        for n, it_ in enumerate(mine):
            it_["bh"] = bh[:, n * LANES:(n + 1) * LANES]

    for it_ in items:
        it_["a"] = jnp.where(msk_ref[it_["z"], nlev] > 0.0, _dot_nt(stack(it_["qq"]), it_["kk"].astype(BF16)), 0.0)
    for li, h in enumerate(SCAN_LEVELS):
        upper = (row & (2 * h - 1)) >= h
        for it_ in items:
            z = it_["z"]
            qrow = jnp.logical_not(upper) if z == 1 else upper
            d = it_["bc"] - it_["bh"][li * c:(li + 1) * c]
            e = jnp.exp(jnp.where(qrow, d, -d))
            zz = jnp.where(qrow, it_["qq"], it_["kk"]) * e
            it_["a"] = jnp.where(msk_ref[z, li] > 0.0, _dot_nt(stack(zz), zz.astype(BF16)), it_["a"])

    for it_ in items:
        z = it_["z"]
        ab = it_["a"].astype(BF16)
        vb = v_refs[z][0, it_["sl"], :].astype(BF16)
        it_["o"] = jnp.concatenate(
            [_dot(ab[hd * c:(hd + 1) * c], vb[:, hd * LANES:(hd + 1) * LANES]) for hd in range(nh)], 1)
        bc = it_["bc"]
        blast = bc[0:1] if z == 1 else bc[c - 1:c]
        it_["qin"] = (it_["qq"] * jnp.exp(bc)).astype(BF16)
        upd = _dot_tn(vb, (it_["kk"] * jnp.exp(blast - bc)).astype(BF16))
        if nh == 2:
            r2 = lax.broadcasted_iota(jnp.int32, (2 * LANES, LANES), 0)
            l2 = lax.broadcasted_iota(jnp.int32, (2 * LANES, LANES), 1)
            upd = jnp.where((r2 < LANES) == (l2 < 64), upd, 0.0)
        it_["upd"] = upd
        it_["dec"] = jnp.exp(blast)

    for z in range(2):
        st = st_refs[z][...]
        for it_ in items:
            if it_["z"] != z:
                continue
            o_refs[z][0, it_["sl"], :] = it_["o"] + _dot_nt(it_["qin"], st.astype(BF16))
            st = st * it_["dec"] + it_["upd"]
        st_refs[z][...] = st

    @pl.when(pl.program_id(2) == pl.num_programs(2) - 1)
    def _():
        for z in range(2):
            fin_ref[0, 0, z] = st_refs[z][...]


def _scan_call(mode, p, init, aux):
    b, l, _ = p.shape
    nh = 2 if mode == "gla" else 1
    ninst = 2 if mode == "gla" else HG_HEADS
    t = min(l, 512)
    nb = l // t
    w = LANES * nh

    def col(name, width, z, extra=0):
        base = (OFF[name] + extra) // width
        if z == 0:
            return pl.BlockSpec((1, t, width), lambda bi, n, i: (bi, i, base + n))
        return pl.BlockSpec((1, t, width), lambda bi, n, i: (bi, nb - 1 - i, base + n))

    def fixed(name, z):
        blk = OFF[name] // LANES
        if z == 0:
            return pl.BlockSpec((1, t, LANES), lambda bi, n, i: (bi, i, blk))
        return pl.BlockSpec((1, t, LANES), lambda bi, n, i: (bi, nb - 1 - i, blk))

    if mode == "gla":
        in_specs = [col("a_q", LANES, 0), col("a_q", LANES, 1), col("a_k", LANES, 0), col("a_k", LANES, 1),
                    col("a_v", w, 0), col("a_v", w, 1), fixed("a_lr", 0), fixed("a_lr", 1),
                    pl.BlockSpec((2, 1, LANES, LANES), lambda bi, n, i: (0, n, 0, 0)),
                    pl.BlockSpec((2, 1, 1, LANES), lambda bi, n, i: (0, n, 0, 0))]
        args = [p] * 8 + [aux[0], aux[1]]
    else:
        fw = HG_HEADS * HG_DK
        in_specs = [col("d_q", LANES, 0), col("d_q", LANES, 1), col("d_f", LANES, 0), col("d_f", LANES, 1, fw),
                    col("d_i", LANES, 0), col("d_i", LANES, 1),
                    pl.BlockSpec((2, 1, LANES), lambda bi, n, i: (0, 0, n))]
        args = [p] * 6 + [aux[0]]
    tri, sel, msk = _scan_consts(nh)
    in_specs += [pl.BlockSpec((1, 1, 2, w, LANES), lambda bi, n, i: (bi, n, 0, 0, 0)),
                 pl.BlockSpec(tri.shape, lambda bi, n, i: (0, 0, 0)),
                 pl.BlockSpec(sel.shape, lambda bi, n, i: (0, 0, 0)),
                 pl.BlockSpec(msk.shape, lambda bi, n, i: (0, 0, 0, 0))]
    args += [init, tri, sel, msk]
    return pl.pallas_call(
        functools.partial(_scan_kernel, mode=mode, t=t),
        grid=(b, ninst, nb),
        in_specs=in_specs,
        out_specs=[pl.BlockSpec((1, t, w), lambda bi, n, i: (bi, i, n)),
                   pl.BlockSpec((1, t, w), lambda bi, n, i: (bi, nb - 1 - i, n)),
                   pl.BlockSpec((1, 1, 2, w, LANES), lambda bi, n, i: (bi, n, 0, 0, 0))],
        out_shape=[jax.ShapeDtypeStruct((b, l, ninst * w), F32),
                   jax.ShapeDtypeStruct((b, l, ninst * w), F32),
                   jax.ShapeDtypeStruct((b, ninst, 2, w, LANES), F32)],
        scratch_shapes=[pltpu.VMEM((w, LANES), F32), pltpu.VMEM((w, LANES), F32)],
        compiler_params=_cparams(("parallel", "parallel", "arbitrary")),
        name=f"scan_{mode}",
    )(*args)


def _recurrent_branch(mode, p, pc, aux):
    b = p.shape[0]
    nh = 2 if mode == "gla" else 1
    ninst = 2 if mode == "gla" else HG_HEADS
    zero = jnp.zeros((b, ninst, 2, LANES * nh, LANES), F32)
    oc_f, oc_b, sc = _scan_call(mode, pc, zero, aux)
    o_f, o_b, _ = _scan_call(mode, p, sc, aux)
    return (o_f, o_b), (oc_f, oc_b)


def _na_bias_table(rpb):
    cols = np.arange(GRID_W)
    cs = np.clip(cols - NA_WIN_C // 2, 0, GRID_W - NA_WIN_C)
    kc = np.arange(GRID_W)
    valid = (kc[None, :] >= cs[:, None]) & (kc[None, :] < cs[:, None] + NA_WIN_C)
    col_idx = np.clip(kc[None, :] - cols[:, None] + NA_WIN_C - 1, 0, 2 * NA_WIN_C - 2)
    tiles = jnp.where(valid[None, None], rpb[:, :, col_idx], -jnp.inf)
    nrow = 2 * NA_WIN_R - 1
    tiles = jnp.concatenate([tiles, jnp.full((NA_HEADS, 1, GRID_W, GRID_W), -jnp.inf, F32)], 1)
    rl = np.arange(NA_RB)[:, None]
    kl = np.arange(NA_KR)[None, :]
    spec = ((np.zeros_like(rl), kl - rl + NA_WIN_R - 1),
            (rl, kl - rl + NA_WIN_R // 2 - 1),
            (np.full_like(rl, NA_RB - 1), kl - rl))
    tidx = np.stack([np.where((kl >= lo) & (kl < lo + NA_WIN_R), off, nrow) for lo, off in spec])
    assert tidx.min() >= 0 and tidx.max() <= nrow
    t = tiles[:, tidx]
    t = jnp.transpose(t, (0, 1, 2, 4, 3, 5))
    return t.reshape(NA_HEADS, 3, NA_RB * GRID_W, NA_KR * GRID_W).astype(F32)


def _softmax_av(parts, extra=None):
    m = functools.reduce(jnp.maximum, [jnp.max(s, axis=-1, keepdims=True) for s, _ in parts])
    if extra is not None:
        m = jnp.maximum(m, extra)
    den = 0.0
    acc = 0.0
    for s, v in parts:
        e = jnp.exp(s - m)
        den = den + jnp.sum(e, axis=-1, keepdims=True)
        acc = acc + _dot(e.astype(BF16), v)
    if extra is not None:
        den = den + jnp.exp(extra - m)
    return acc / den


def _na_kernel(q_ref, k_ref, v_ref, kc_ref, vc_ref, bias_ref, o_ref, *, rows):
    kc = kc_ref[0].astype(BF16)
    vc = vc_ref[0].astype(BF16)
    qn, kn = NA_RB * GRID_W, NA_KR * GRID_W
    nblk = rows // NA_RB
    lane = lax.broadcasted_iota(jnp.int32, (qn, LANES), 1)

    def body(bi, carry):
        r0 = bi * NA_RB
        w0 = jnp.clip(r0 - NA_WIN_R // 2, 0, rows - NA_KR)
        cls = jnp.where(bi == 0, 0, jnp.where(bi == nblk - 1, 2, 1))
        qs = pl.ds(pl.multiple_of(r0 * GRID_W, qn), qn)
        ks = pl.ds(pl.multiple_of(w0 * GRID_W, GRID_W), kn)
        q = q_ref[0, qs, :] * (NA_DH ** -0.5)
        kw = k_ref[0, ks, :].astype(BF16)
        vw = v_ref[0, ks, :].astype(BF16)
        outs = []
        for hd in range(2):
            qm = jnp.where((lane < 64) if hd == 0 else (lane >= 64), q, 0.0).astype(BF16)
            s_loc = _dot_nt(qm, kw) + bias_ref[hd, cls]
            s_ctx = _dot_nt(qm, kc)
            outs.append(_softmax_av([(s_loc, vw), (s_ctx, vc)]))
        o_ref[0, qs, :] = jnp.where(lane < 64, outs[0], outs[1])
        return carry

    lax.fori_loop(0, nblk, body, 0)


def _na_attention(p, pc, bias):
    b, l, _ = p.shape
    lc = pc.shape[1]
    rows = l // GRID_W
    assert rows % NA_RB == 0 and rows >= NA_KR + 1
    npair = NA_HEADS // 2

    def col(name):
        base = OFF[name] // LANES
        return pl.BlockSpec((1, l, LANES), lambda bi, n: (bi, 0, base + n))

    def ccol(name):
        base = OFF[name] // LANES
        return pl.BlockSpec((1, lc, LANES), lambda bi, n: (bi, 0, base + n))

    return pl.pallas_call(
        functools.partial(_na_kernel, rows=rows),
        grid=(b, npair),
        in_specs=[col("b_q"), col("b_k"), col("b_v"), ccol("b_k"), ccol("b_v"),
                  pl.BlockSpec((2, 3, NA_RB * GRID_W, NA_KR * GRID_W), lambda bi, n: (n, 0, 0, 0))],
        out_specs=pl.BlockSpec((1, l, LANES), lambda bi, n: (bi, 0, n)),
        out_shape=jax.ShapeDtypeStruct((b, l, NA_HEADS * NA_DH), F32),
        compiler_params=_cparams(("parallel", "parallel")),
        name="na_attention",
    )(p, p, p, pc, pc, bias)


def _ctx_attn_kernel(q_ref, k_ref, v_ref, sink_ref, o_ref, *, use_sink):
    lc = q_ref.shape[1]
    lane = lax.broadcasted_iota(jnp.int32, (lc, LANES), 1)
    q = q_ref[0] * (NA_DH ** -0.5)
    kb = k_ref[0].astype(BF16)
    vb = v_ref[0].astype(BF16)
    outs = []
    for hd in range(2):
        qm = jnp.where((lane < 64) if hd == 0 else (lane >= 64), q, 0.0).astype(BF16)
        s = _dot_nt(qm, kb)
        extra = sink_ref[0, hd][:, 0:1] if use_sink else None
        outs.append(_softmax_av([(s, vb)], extra))
    o_ref[0] = jnp.where(lane < 64, outs[0], outs[1])


def _ctx_attention(pc, qname, kname, vname, kv_per_q, sink_tab):
    b, lc, _ = pc.shape
    nblk = 4
    use_sink = sink_tab is not None
    if sink_tab is None:
        sink_tab = jnp.zeros((nblk, 2, lc, LANES), F32)
    qb, kb, vb = OFF[qname] // LANES, OFF[kname] // LANES, OFF[vname] // LANES
    kmul = 1 if kv_per_q else 0
    return pl.pallas_call(
        functools.partial(_ctx_attn_kernel, use_sink=use_sink),
        grid=(b, nblk),
        in_specs=[pl.BlockSpec((1, lc, LANES), lambda bi, n: (bi, 0, qb + n)),
                  pl.BlockSpec((1, lc, LANES), lambda bi, n: (bi, 0, kb + kmul * n)),
                  pl.BlockSpec((1, lc, LANES), lambda bi, n: (bi, 0, vb + kmul * n)),
                  pl.BlockSpec((1, 2, lc, LANES), lambda bi, n: (n, 0, 0, 0))],
        out_specs=pl.BlockSpec((1, lc, LANES), lambda bi, n: (bi, 0, n)),
        out_shape=jax.ShapeDtypeStruct((b, lc, nblk * LANES), F32),
        compiler_params=_cparams(("parallel", "parallel")),
        name="ctx_attention",
    )(pc, pc, pc, sink_tab)


def _rope_tables(l):
    nf = WA_DH // 4
    pos = np.arange(l)
    inv = ROPE_BASE ** (-np.arange(nf, dtype=np.float32) / nf)
    d = np.arange(WA_DH)
    p = np.where(d[None, :] < WA_DH // 2, (pos // GRID_W)[:, None], (pos % GRID_W)[:, None]).astype(np.float32)
    ang = p * inv[d % nf][None, :].astype(np.float32)
    cos, sin = np.cos(ang), np.sin(ang)
    first = (d % (2 * nf)) < nf
    ta = np.where(first[None, :], -sin, 0.0)
    tb = np.where(first[None, :], 0.0, sin)
    tile = lambda t: jnp.asarray(np.tile(t, (1, LANES // WA_DH)), F32)
    return tile(cos), tile(ta), tile(tb)


def _rope_kernel(q_ref, k_ref, c_ref, a_ref, b_ref, qo_ref, ko_ref):
    c, a, b = c_ref[...], a_ref[...], b_ref[...]
    nf = WA_DH // 4

    def rot(u):
        return u * c + pltpu.roll(u, LANES - nf, 1) * a + pltpu.roll(u, nf, 1) * b

    for j in range(q_ref.shape[2] // LANES):
        qo_ref[0, :, j * LANES:(j + 1) * LANES] = rot(q_ref[0, :, j * LANES:(j + 1) * LANES])
    ko_ref[0] = rot(k_ref[0])


def _rope(p, tabs):
    b, l, _ = p.shape
    t = min(l, 512)
    qw = WA_HEADS * WA_DH
    tspec = pl.BlockSpec((t, LANES), lambda bi, i: (i, 0))
    return pl.pallas_call(
        _rope_kernel,
        grid=(b, l // t),
        in_specs=[pl.BlockSpec((1, t, qw), lambda bi, i: (bi, i, OFF["c_q"] // qw)),
                  pl.BlockSpec((1, t, LANES), lambda bi, i: (bi, i, OFF["c_k"] // LANES)),
                  tspec, tspec, tspec],
        out_specs=[pl.BlockSpec((1, t, qw), lambda bi, i: (bi, i, 0)),
                   pl.BlockSpec((1, t, LANES), lambda bi, i: (bi, i, 0))],
        out_shape=[jax.ShapeDtypeStruct((b, l, qw), F32), jax.ShapeDtypeStruct((b, l, LANES), F32)],
        compiler_params=_cparams(("parallel", "parallel")),
        name="rope",
    )(p, p, *tabs)


def _wa_kernel(q_ref, k_ref, v_ref, kc_ref, vc_ref, sink_ref, o_ref, *, l):
    n = pl.program_id(1)
    blk = WA_BLOCK
    span = 3 * blk
    start = jnp.clip((n - 1) * blk, 0, l - span)
    ks = pl.ds(pl.multiple_of(start, blk), span)
    kw = k_ref[0, ks, :].astype(BF16)
    vw = v_ref[0, ks, :].astype(BF16)
    kc = kc_ref[0].astype(BF16)
    vc = vc_ref[0].astype(BF16)
    ngrp = WA_HEADS // 2
    lane = lax.broadcasted_iota(jnp.int32, (blk, LANES), 1)
    ii = lax.broadcasted_iota(jnp.int32, (ngrp * blk, span), 0) & (blk - 1)
    jj = lax.broadcasted_iota(jnp.int32, (ngrp * blk, span), 1)
    dist = (jj + start) - (ii + n * blk)
    valid = jnp.abs(dist) <= WA_RADIUS
    halves = []
    for half in range(2):
        keep = (lane < 64) if half == 0 else (lane >= 64)
        qs = jnp.concatenate(
            [jnp.where(keep, q_ref[0, :, j * LANES:(j + 1) * LANES] * (WA_DH ** -0.5), 0.0) for j in range(ngrp)],
            0).astype(BF16)
        s_loc = jnp.where(valid, _dot_nt(qs, kw), -jnp.inf)
        s_ctx = _dot_nt(qs, kc)
        halves.append(_softmax_av([(s_loc, vw), (s_ctx, vc)], sink_ref[half][:, 0:1]))
    for j in range(ngrp):
        o_ref[0, :, j * LANES:(j + 1) * LANES] = jnp.where(lane < 64, halves[0][j * blk:(j + 1) * blk],
                                                           halves[1][j * blk:(j + 1) * blk])


def _wa_attention(qr, kr, p, pc, sink_rows):
    b, l, qw = qr.shape
    lc = pc.shape[1]
    return pl.pallas_call(
        functools.partial(_wa_kernel, l=l),
        grid=(b, l // WA_BLOCK),
        in_specs=[pl.BlockSpec((1, WA_BLOCK, qw), lambda bi, n: (bi, n, 0)),
                  pl.BlockSpec((1, l, LANES), lambda bi, n: (bi, 0, 0)),
                  pl.BlockSpec((1, l, LANES), lambda bi, n: (bi, 0, OFF["c_v"] // LANES)),
                  pl.BlockSpec((1, lc, LANES), lambda bi, n: (bi, 0, OFF["c_k"] // LANES)),
                  pl.BlockSpec((1, lc, LANES), lambda bi, n: (bi, 0, OFF["c_v"] // LANES)),
                  pl.BlockSpec(sink_rows.shape, lambda bi, n: (0, 0, 0))],
        out_specs=pl.BlockSpec((1, WA_BLOCK, qw), lambda bi, n: (bi, n, 0)),
        out_shape=jax.ShapeDtypeStruct((b, l, qw), F32),
        compiler_params=_cparams(("parallel", "arbitrary")),
        name="wa_attention",
    )(qr, kr, p, pc, pc, sink_rows)


def _rms_gate(o, og):
    outs = []
    for hd in range(o.shape[1] // LANES):
        oh = o[:, hd * LANES:(hd + 1) * LANES]
        r = lax.rsqrt(jnp.mean(oh * oh, axis=-1, keepdims=True) + 1e-6)
        outs.append(oh * r * _silu(og[:, hd * LANES:(hd + 1) * LANES]))
    return jnp.concatenate(outs, 1)


def _merge_kernel(oaf_ref, oab_ref, ag_ref, ob_ref, oc_ref, odf_ref, odb_ref, dg_ref, gate_ref, x_ref, mod_ref,
                  wbr_ref, wout_ref, lng_ref, lnb_ref, o_ref):
    branches = (_rms_gate(oaf_ref[0] + oab_ref[0], ag_ref[0]), ob_ref[0], oc_ref[0],
                _rms_gate(odf_ref[0] + odb_ref[0], dg_ref[0]))
    acc = 0.0
    for i, br in enumerate(branches):
        y = _dot(br.astype(BF16), wbr_ref[i])
        acc = acc + _sigmoid(gate_ref[0, :, i * D_MODEL:(i + 1) * D_MODEL]) * y
    mix = _dot(acc.astype(BF16), wout_ref[...])
    m = mod_ref[0]
    z = ALPHA * x_ref[...] + m[2:3] * mix
    o_ref[...] = _layer_norm(z, lng_ref[...], lnb_ref[...])


def _merge_out_ln(o_a, o_b, o_c, o_d, p, x, mod, wbr, wout, lng, lnb):
    b, l, _ = p.shape
    d = x.shape[1]
    tm = min(l, 256)
    nt = l // tm
    gw = N_BRANCH * D_MODEL
    bspec = pl.BlockSpec((1, tm, BRANCH_W), lambda bi, i: (bi, i, 0))

    def pcol(name):
        blk = OFF[name] // BRANCH_W
        return pl.BlockSpec((1, tm, BRANCH_W), lambda bi, i: (bi, i, blk))

    return pl.pallas_call(
        _merge_kernel,
        grid=(b, l // tm),
        in_specs=[bspec, bspec, pcol("a_g"), bspec, bspec, bspec, bspec, pcol("d_g"),
                  pl.BlockSpec((1, tm, gw), lambda bi, i: (bi, i, OFF["gate"] // gw)),
                  pl.BlockSpec((tm, d), lambda bi, i: (bi * nt + i, 0)),
                  pl.BlockSpec((1, 8, d), lambda bi, i: (bi, 0, 0)),
                  pl.BlockSpec((N_BRANCH, BRANCH_W, d), lambda bi, i: (0, 0, 0)),
                  pl.BlockSpec((d, d), lambda bi, i: (0, 0)),
                  pl.BlockSpec((1, d), lambda bi, i: (0, 0)),
                  pl.BlockSpec((1, d), lambda bi, i: (0, 0))],
        out_specs=pl.BlockSpec((tm, d), lambda bi, i: (bi * nt + i, 0)),
        out_shape=jax.ShapeDtypeStruct((b * l, d), F32),
        compiler_params=_cparams(("parallel", "parallel")),
        name="merge_out_ln",
    )(o_a[0], o_a[1], p, o_b, o_c, o_d[0], o_d[1], p, p, x, mod, wbr, wout, lng, lnb)


def _dot3(a, w):
    a1 = a.astype(BF16)
    a2 = (a - a1.astype(F32)).astype(BF16)
    w1 = w.astype(BF16)
    w2 = (w - w1.astype(F32)).astype(BF16)
    return _dot(a1, w1) + (_dot(a1, w2) + _dot(a2, w1))


def _group_kernel(x_ref, mod_ref, wr_ref, br_ref, o_ref):
    m = mod_ref[0]
    h = x_ref[...] * (1.0 + m[4:5]) + m[3:4]
    lane = lax.broadcasted_iota(jnp.int32, (h.shape[0], LANES), 1)
    gl = jnp.where(lane < N_GROUPS, _dot3(h, wr_ref[...]) + br_ref[...], -jnp.inf)
    gmax = jnp.max(gl, axis=-1, keepdims=True)
    o_ref[...] = jnp.min(jnp.where(gl == gmax, lane, LANES), axis=-1, keepdims=True)


def _moe_groups(x, b, l, mod, wr, br):
    d = x.shape[1]
    tm = min(l, 1024)
    nt = l // tm
    return pl.pallas_call(
        _group_kernel,
        grid=(b, nt),
        in_specs=[pl.BlockSpec((tm, d), lambda bi, i: (bi * nt + i, 0)),
                  pl.BlockSpec((1, 8, d), lambda bi, i: (bi, 0, 0)),
                  pl.BlockSpec((d, LANES), lambda bi, i: (0, 0)),
                  pl.BlockSpec((1, LANES), lambda bi, i: (0, 0))],
        out_specs=pl.BlockSpec((tm, 1), lambda bi, i: (bi * nt + i, 0)),
        out_shape=jax.ShapeDtypeStruct((b * l, 1), jnp.int32),
        compiler_params=_cparams(("parallel", "parallel")),
        name="moe_groups",
    )(x, mod, wr, br)


def _moe_plan(gidx, n):
    tmax = n // MOE_TM + N_GROUPS
    key = gidx.reshape(n)
    order = jnp.argsort(key, stable=True).astype(jnp.int32)
    counts = jnp.sum(key[:, None] == jnp.arange(N_GROUPS, dtype=jnp.int32)[None, :], axis=0).astype(jnp.int32)
    tiles = (counts + MOE_TM - 1) // MOE_TM
    tile_end = jnp.cumsum(tiles)
    ntiles = tile_end[-1]
    t = jnp.arange(tmax, dtype=jnp.int32)
    tc = jnp.minimum(t, ntiles - 1)
    seg = jnp.minimum(jnp.searchsorted(tile_end, tc, side="right"), N_GROUPS - 1).astype(jnp.int32)
    in_seg = tc - (tile_end[seg] - tiles[seg])
    cstart = jnp.cumsum(counts) - counts
    off = in_seg[:, None] * MOE_TM + jnp.arange(MOE_TM, dtype=jnp.int32)[None, :]
    valid = jnp.logical_and(off < counts[seg][:, None], (t < ntiles)[:, None])
    src = cstart[seg][:, None] + jnp.minimum(off, counts[seg][:, None] - 1)
    rows_in = order[jnp.clip(src, 0, n - 1)]
    rows_out = jnp.where(valid, rows_in, n + jnp.arange(MOE_TM, dtype=jnp.int32)[None, :])
    return seg, rows_in, rows_out


def _moe_tile_kernel(tg_ref, rin_ref, rnext_ref, rout_ref, rvec_ref, x_hbm, mod_ref, wr_ref, br_ref, wgu_ref,
                     wd_ref, lng_ref, lnb_ref, o_hbm, xbuf, ybuf, sem_in, sem_out, *, l):
    t = pl.program_id(0)
    last = pl.num_programs(0) - 1
    tm = MOE_TM
    slot = lax.rem(t, 2)

    def row_in(rows_ref, r, s):
        return pltpu.make_async_copy(x_hbm.at[pl.ds(rows_ref[0, 0, r], 1)], xbuf.at[s, pl.ds(r, 1)], sem_in.at[s])

    def row_out(r):
        return pltpu.make_async_copy(ybuf.at[pl.ds(r, 1)], o_hbm.at[pl.ds(rout_ref[0, 0, r], 1)], sem_out)

    @pl.when(t == 0)
    def _():
        for r in range(tm):
            row_in(rin_ref, r, 0).start()
        ybuf[...] = jnp.zeros_like(ybuf)
        spare = pltpu.make_async_copy(ybuf, o_hbm.at[pl.ds(o_hbm.shape[0] - tm, tm)], sem_out)
        spare.start()
        spare.wait()

    for r in range(tm):
        row_in(rin_ref, r, slot).wait()
    for r in range(tm):
        row_in(rnext_ref, r, 1 - slot).start()

    g = tg_ref[t]
    x = xbuf[slot]
    tok = rvec_ref[0]
    bidx = jnp.zeros((tm, 1), jnp.int32)
    nb = mod_ref.shape[0]
    for k in range(1, nb):
        bidx = bidx + (tok >= k * l).astype(jnp.int32)

    def mod_row(j):
        out = mod_ref[0, j:j + 1, :]
        for k in range(1, nb):
            out = jnp.where(bidx == k, mod_ref[k, j:j + 1, :], out)
        return out

    h = x * (1.0 + mod_row(4)) + mod_row(3)
    lane = lax.broadcasted_iota(jnp.int32, (tm, LANES), 1)
    ninf = -jnp.inf
    lg = _dot3(h, wr_ref[...]) + br_ref[...]
    gl = jnp.where(lane < N_GROUPS, lg, ninf)
    gmax = jnp.max(gl, axis=-1, keepdims=True)
    gsel = jnp.sum(jnp.where(lane == g, lg, 0.0), axis=-1, keepdims=True)
    gw = jnp.exp(gsel - gmax) / jnp.sum(jnp.exp(gl - gmax), axis=-1, keepdims=True)
    e0 = N_GROUPS + g * EXPERTS_PER_GROUP
    ingrp = jnp.logical_and(lane >= e0, lane < e0 + EXPERTS_PER_GROUP)
    e1 = jnp.where(ingrp, lg, ninf)
    m1 = jnp.max(e1, axis=-1, keepdims=True)
    i1 = jnp.min(jnp.where(e1 == m1, lane, LANES), axis=-1, keepdims=True)
    e2 = jnp.where(lane == i1, ninf, e1)
    m2 = jnp.max(e2, axis=-1, keepdims=True)
    i2 = jnp.min(jnp.where(e2 == m2, lane, LANES), axis=-1, keepdims=True)
    tt = jnp.exp(m2 - m1)
    wts = gw * (jnp.where(lane == i1, 1.0 / (1.0 + tt), 0.0) + jnp.where(lane == i2, tt / (1.0 + tt), 0.0))

    hb = h.astype(BF16)
    acc = jnp.zeros((tm, D_MODEL), F32)
    for j in range(EXPERTS_PER_GROUP):
        wcol = jnp.sum(jnp.where(lane == e0 + j, wts, 0.0), axis=-1, keepdims=True)
        gu = _dot(hb, wgu_ref[j])
        hid = _silu(gu[:, :EXPERT_HIDDEN]) * gu[:, EXPERT_HIDDEN:] * wcol
        acc = acc + _dot(hid.astype(BF16), wd_ref[j])
    y = _layer_norm(ALPHA * x + mod_row(5) * acc, lng_ref[...], lnb_ref[...])

    @pl.when(t > 0)
    def _():
        for r in range(tm):
            row_out(r).wait()

    ybuf[...] = y
    for r in range(tm):
        row_out(r).start()

    @pl.when(t == last)
    def _():
        for r in range(tm):
            row_out(r).wait()
        for r in range(tm):
            row_in(rnext_ref, r, 1 - slot).wait()


def _moe_ln(x, b, l, mod, wr, br, wgu, wd, lng, lnb):
    d = x.shape[1]
    n = b * l
    gidx = _moe_groups(x, b, l, mod, wr, br)
    tile_group, rows_in, rows_out = _moe_plan(gidx, n)
    tmax = rows_in.shape[0]
    cst = lambda shape: pl.BlockSpec(shape, lambda t, tg: tuple(0 for _ in shape))
    smem_rows = lambda fn: pl.BlockSpec((1, 1, MOE_TM), fn, memory_space=pltpu.SMEM)
    grid_spec = pltpu.PrefetchScalarGridSpec(
        num_scalar_prefetch=1,
        grid=(tmax,),
        in_specs=[smem_rows(lambda t, tg: (t, 0, 0)),
                  smem_rows(lambda t, tg: (jnp.minimum(t + 1, tmax - 1), 0, 0)),
                  smem_rows(lambda t, tg: (t, 0, 0)),
                  pl.BlockSpec((1, MOE_TM, 1), lambda t, tg: (t, 0, 0)),
                  pl.BlockSpec(memory_space=pl.ANY),
                  cst(mod.shape), cst((d, LANES)), cst((1, LANES)),
                  pl.BlockSpec((EXPERTS_PER_GROUP, d, 2 * EXPERT_HIDDEN), lambda t, tg: (tg[t], 0, 0)),
                  pl.BlockSpec((EXPERTS_PER_GROUP, EXPERT_HIDDEN, d), lambda t, tg: (tg[t], 0, 0)),
                  cst((1, d)), cst((1, d))],
        out_specs=pl.BlockSpec(memory_space=pl.ANY),
        scratch_shapes=[pltpu.VMEM((2, MOE_TM, d), F32), pltpu.VMEM((MOE_TM, d), F32),
                        pltpu.SemaphoreType.DMA((2,)), pltpu.SemaphoreType.DMA],
    )
    rin3 = rows_in.reshape(tmax, 1, MOE_TM)
    return pl.pallas_call(
        functools.partial(_moe_tile_kernel, l=l),
        grid_spec=grid_spec,
        out_shape=jax.ShapeDtypeStruct((n + MOE_TM, d), F32),
        compiler_params=_cparams(("arbitrary",)),
        name="moe_ln",
    )(tile_group, rin3, rin3, rows_out.reshape(tmax, 1, MOE_TM), rows_in.reshape(tmax, MOE_TM, 1), x, mod, wr, br,
      wgu, wd, lng, lnb)


def _pad_cols(w, n):
    return jnp.pad(w, ((0, 0), (0, n - w.shape[1])))


def _wa_head_pairs(w, axis):
    shp = w.shape
    w = w.reshape(shp[:axis] + (2, WA_HEADS // 2, WA_DH) + shp[axis + 1:])
    w = jnp.swapaxes(w, axis, axis + 1)
    return w.reshape(shp)


def _prep_w_in(w):
    segs = []
    for name in _NEW_ORDER:
        s = w[:, _ORIG_OFF[name]:_ORIG_OFF[name] + _SIZES[name]]
        if name == "c_q":
            s = _wa_head_pairs(s, 1)
        segs.append(s)
    return _pad_cols(jnp.concatenate(segs, 1), NP).astype(BF16)


def _prep_gla_gate(lr_w, lr_b):
    w = jnp.zeros((2, 2, LANES, LANES), F32)
    for z in range(2):
        for n in range(2):
            w = w.at[z, n, z * GLA_RANK:(z + 1) * GLA_RANK, :].set(lr_w[z][:, n * LANES:(n + 1) * LANES])
    return w.astype(BF16), lr_b.reshape(2, 2, 1, LANES)


def _sink_rows(sink, rows_per_head):
    s = sink.reshape(2, 4)
    return jnp.broadcast_to(s[:, :, None, None], (2, 4, rows_per_head, LANES)).reshape(2, 4 * rows_per_head, LANES)


def kernel(x, c, ctx, c_ctx, w_mod, b_mod, w_in, gla_lr_w, gla_lr_b, hg_gamma, na_rpb, wa_sink, w_branch, w_out,
           ln_g, ln_b, moe_w_group, moe_b_group, moe_w_expert, moe_b_expert, moe_w_gate, moe_w_up, moe_w_down):
    b, l, d = x.shape
    lc = ctx.shape[1]
    assert d == D_MODEL and b + 1 <= 8 and l % 512 == 0 and l >= 3 * WA_BLOCK and lc % SCAN_C == 0

    cc = jnp.concatenate([c, c_ctx[None], jnp.zeros((8 - b - 1, d), F32)], 0)
    mods = _modulation(cc, w_mod, b_mod)
    cum = jnp.cumsum(jax.nn.softmax(hg_gamma.astype(F32), axis=1), axis=1)
    lower_bounds = cum - cum[:, :1]
    rope_tabs = _rope_tables(l)

    x = x.reshape(b * l, d)
    cx = ctx.reshape(b * lc, d)
    for li in range(DEPTH):
        with_ctx = li < DEPTH - 1
        mod6 = mods[li].reshape(8, 6, d)
        mod = jnp.pad(mod6[:b], ((0, 0), (0, 2), (0, 0)))
        modc = jnp.broadcast_to(jnp.pad(mod6[b], ((0, 2), (0, 0)))[None], (b, 8, d))
        w_in_p = _prep_w_in(w_in[li])
        p = _in_proj(x, b, l, mod, w_in_p)
        pc = _in_proj(cx, b, lc, modc, w_in_p)

        gla_aux = _prep_gla_gate(gla_lr_w[li], gla_lr_b[li])
        o_a, oc_a = _recurrent_branch("gla", p, pc, gla_aux)
        hg_aux = (lower_bounds[:, li].reshape(2, 1, HG_HEADS * HG_DK),)
        o_d, oc_d = _recurrent_branch("hgrn", p, pc, hg_aux)

        o_b = _na_attention(p, pc, _na_bias_table(na_rpb[li]))
        qr, kr = _rope(p, rope_tabs)
        o_c = _wa_attention(qr, kr, p, pc, _sink_rows(wa_sink[li], WA_BLOCK))

        wbr = jnp.stack([w_branch[li, 0], w_branch[li, 1], _wa_head_pairs(w_branch[li, 2], 0),
                         w_branch[li, 3]]).astype(BF16)
        wout = w_out[li].astype(BF16)
        lng0, lnb0 = ln_g[li, 0][None], ln_b[li, 0][None]
        x = _merge_out_ln(o_a, o_b, o_c, o_d, p, x, mod, wbr, wout, lng0, lnb0)

        wr = _pad_cols(jnp.concatenate([moe_w_group[li], moe_w_expert[li]], 1), LANES)
        br = _pad_cols(jnp.concatenate([moe_b_group[li], moe_b_expert[li]])[None], LANES)
        wgu = jnp.concatenate([moe_w_gate[li].astype(BF16), moe_w_up[li].astype(BF16)], -1)
        wgu = wgu.reshape(N_EXPERTS, d, 2 * EXPERT_HIDDEN)
        wd = moe_w_down[li].reshape(N_EXPERTS, EXPERT_HIDDEN, d).astype(BF16)
        lng1, lnb1 = ln_g[li, 1][None], ln_b[li, 1][None]
        moe = functools.partial(_moe_ln, wr=wr, br=br, wgu=wgu, wd=wd, lng=lng1, lnb=lnb1)
        if with_ctx:
            oc_b = _ctx_attention(pc, "b_q", "b_k", "b_v", True, None)
            oc_c = _ctx_attention(pc, "c_q", "c_k", "c_v", False,
                                  jnp.broadcast_to(wa_sink[li].reshape(2, 4).T[:, :, None, None], (4, 2, lc, LANES)))
            cx = _merge_out_ln(oc_a, oc_b, oc_c, oc_d, pc, cx, modc, wbr, wout, lng0, lnb0)
            cx = moe(cx, b, lc, modc)
        x = moe(x, b, l, mod)
    return x[:b * l].reshape(b, l, d)
```

```python
import functools

import numpy as np
import jax
import jax.numpy as jnp
from jax import lax
from jax.experimental import pallas as pl
from jax.experimental.pallas import tpu as pltpu

F32 = jnp.float32
BF16 = jnp.bfloat16
HIGHEST = lax.Precision.HIGHEST

LANES = 128
VMEM_LIMIT = 56 * 1024 * 1024

D_MODEL = 1024
DEPTH = 2
GRID_W = 64
GLA_HEADS, GLA_DK, GLA_DV, GLA_RANK, GLA_TAU = 4, 64, 128, 16, 16.0
NA_HEADS, NA_DH, NA_WIN_R, NA_WIN_C = 8, 64, 8, 16
WA_HEADS, WA_KV_HEADS, WA_DH, WA_RADIUS, WA_BLOCK = 8, 2, 64, 128, 128
ROPE_BASE = 10000.0
HG_HEADS, HG_DK, HG_DV = 4, 128, 128
N_BRANCH, BRANCH_W = 4, 512
N_GROUPS, EXPERTS_PER_GROUP, EXPERT_HIDDEN = 4, 8, 256
N_EXPERTS = N_GROUPS * EXPERTS_PER_GROUP
LN_EPS = 1e-5
ALPHA = (2.0 * DEPTH) ** 0.25

_IN_COLS = (
    ("a_q", 256), ("a_k", 256), ("a_v", 512), ("a_g", 512), ("a_lr", 32),
    ("b_q", 512), ("b_k", 512), ("b_v", 512),
    ("c_q", 512), ("c_k", 128), ("c_v", 128),
    ("d_q", 512), ("d_f", 1024), ("d_i", 512), ("d_g", 512), ("gate", 4096),
)
_ORIG_OFF = {}
_o = 0
for _n, _s in _IN_COLS:
    _ORIG_OFF[_n] = _o
    _o += _s
D_IN = _o
_NEW_ORDER = ("gate", "a_q", "a_k", "a_v", "a_g", "b_q", "b_k", "b_v", "c_q",
              "d_q", "d_f", "d_i", "d_g", "c_k", "c_v", "a_lr")
_SIZES = dict(_IN_COLS)
OFF = {}
_o = 0
for _n in _NEW_ORDER:
    OFF[_n] = _o
    _o += _SIZES[_n]
PROJ_TN = 1536
NP = ((_o + PROJ_TN - 1) // PROJ_TN) * PROJ_TN

SCAN_C = 128
SCAN_LEVELS = (64, 32, 16, 8, 4, 2, 1)

MOE_TM = 256

NA_RB = 4
NA_KR = NA_RB + NA_WIN_R - 1


def _cparams(sem, vmem=VMEM_LIMIT):
    return pltpu.CompilerParams(dimension_semantics=sem, vmem_limit_bytes=vmem)


def _dot(a, b, **kw):
    return jnp.dot(a, b, preferred_element_type=F32, **kw)


def _dot_nt(a, b):
    return lax.dot_general(a, b, (((1,), (1,)), ((), ())), preferred_element_type=F32)


def _dot_tn(a, b):
    return lax.dot_general(a, b, (((0,), (0,)), ((), ())), preferred_element_type=F32)


def _dot3(a, w):
    a1 = a.astype(BF16)
    a2 = (a - a1.astype(F32)).astype(BF16)
    w1 = w.astype(BF16)
    w2 = (w - w1.astype(F32)).astype(BF16)
    return _dot(a1, w1) + (_dot(a1, w2) + _dot(a2, w1))


def _sigmoid(x):
    return 1.0 / (1.0 + jnp.exp(-x))


def _silu(x):
    return x * _sigmoid(x)


def _log_sigmoid(x):
    return jnp.minimum(x, 0.0) - jnp.log(1.0 + jnp.exp(-jnp.abs(x)))


def _layer_norm(z, g, b):
    mu = jnp.mean(z, axis=-1, keepdims=True)
    zc = z - mu
    var = jnp.mean(zc * zc, axis=-1, keepdims=True)
    return zc * lax.rsqrt(var + LN_EPS) * g + b


def _mod_kernel(c_ref, w_ref, b_ref, o_ref):
    c = c_ref[...]
    o_ref[0] = _dot(_silu(c), w_ref[0], precision=HIGHEST) + b_ref[0]


def _modulation(cc, w_mod, b_mod):
    depth, d, n = w_mod.shape
    tn = 1536
    return pl.pallas_call(
        _mod_kernel,
        grid=(depth, n // tn),
        in_specs=[pl.BlockSpec((8, d), lambda l, j: (0, 0)),
                  pl.BlockSpec((1, d, tn), lambda l, j: (l, 0, j)),
                  pl.BlockSpec((1, 1, tn), lambda l, j: (l, 0, j))],
        out_specs=pl.BlockSpec((1, 8, tn), lambda l, j: (l, 0, j)),
        out_shape=jax.ShapeDtypeStruct((depth, 8, n), F32),
        compiler_params=_cparams(("parallel", "parallel")),
        name="modulation",
    )(cc, w_mod, b_mod.reshape(depth, 1, n))


def _in_proj_kernel(x_ref, mod_ref, w_ref, o_ref, h_ref):
    @pl.when(pl.program_id(2) == 0)
    def _():
        m = mod_ref[0]
        h_ref[...] = (x_ref[...] * (1.0 + m[1:2]) + m[0:1]).astype(BF16)

    o_ref[0] = _dot(h_ref[...], w_ref[...])


def _in_proj(x, b, l, mod, w):
    d = x.shape[1]
    tm = min(l, 1024)
    nt = l // tm
    return pl.pallas_call(
        _in_proj_kernel,
        grid=(b, nt, NP // PROJ_TN),
        in_specs=[pl.BlockSpec((tm, d), lambda bi, i, j: (bi * nt + i, 0)),
                  pl.BlockSpec((1, 8, d), lambda bi, i, j: (bi, 0, 0)),
                  pl.BlockSpec((d, PROJ_TN), lambda bi, i, j: (0, j))],
        out_specs=pl.BlockSpec((1, tm, PROJ_TN), lambda bi, i, j: (bi, i, j)),
        out_shape=jax.ShapeDtypeStruct((b, l, NP), F32),
        scratch_shapes=[pltpu.VMEM((tm, d), BF16)],
        compiler_params=_cparams(("parallel", "parallel", "arbitrary")),
        name="in_proj",
    )(x, mod, w)


def _scan_consts(nh):
    c = SCAN_C
    i = np.arange(c)[:, None]
    j = np.arange(c)[None, :]
    tris, sels, msks = [], [], []
    for reverse in (False, True):
        tris.append((j >= i) if reverse else (j <= i))
        sel, masks = [], []
        for h in SCAN_LEVELS:
            same = (i // (2 * h)) == (j // (2 * h))
            base = (np.arange(c) // (2 * h)) * (2 * h)
            if reverse:
                pos = base + h
                m = same & ((i % (2 * h)) < h) & ((j % (2 * h)) >= h)
            else:
                pos = base + h - 1
                m = same & ((i % (2 * h)) >= h) & ((j % (2 * h)) < h)
            sel.append(j == pos[:, None])
            masks.append(m)
        masks.append(i == j)
        sels.append(np.concatenate(sel, 0))
        msks.append(np.stack([np.tile(m, (nh, 1)) for m in masks]))
    return (jnp.asarray(np.stack(tris), BF16), jnp.asarray(np.stack(sels), BF16),
            jnp.asarray(np.stack(msks), F32))


def _split3(x):
    a = x.astype(BF16)
    r = x - a.astype(F32)
    b = r.astype(BF16)
    c = (r - b.astype(F32)).astype(BF16)
    return jnp.concatenate([a, b, c], 1)


def _scan_kernel(*refs, mode, t):
    nh = 2 if mode == "gla" else 1
    it = iter(refs)
    q_refs, k_refs, v_refs = (next(it), next(it)), (next(it), next(it)), (next(it), next(it))
    if mode == "gla":
        lr_refs = (next(it), next(it))
        wlr_ref, blr_ref = next(it), next(it)
    else:
        lb_ref = next(it)
    init_ref, tri_ref, sel_ref, msk_ref = next(it), next(it), next(it), next(it)
    o_refs = (next(it), next(it))
    fin_ref = next(it)
    st_refs = (next(it), next(it))

    @pl.when(pl.program_id(2) == 0)
    def _():
        for z in range(2):
            st_refs[z][...] = init_ref[0, 0, z]

    c = SCAN_C
    nc = t // c
    nlev = len(SCAN_LEVELS)
    row = lax.broadcasted_iota(jnp.int32, (c, LANES), 0)
    lane = lax.broadcasted_iota(jnp.int32, (c, LANES), 1)

    def stack(x):
        if nh == 1:
            return x.astype(BF16)
        return jnp.concatenate([jnp.where(lane < 64, x, 0.0), jnp.where(lane >= 64, x, 0.0)], 0).astype(BF16)

    order = [[ci if z == 0 else nc - 1 - ci for ci in range(nc)] for z in range(2)]
    items = []
    for z in range(2):
        for cc in order[z]:
            sl = pl.ds(cc * c, c)
            if mode == "gla":
                qq = q_refs[z][0, sl, :] * (GLA_DK ** -0.5)
                kk = k_refs[z][0, sl, :]
                logit = _dot(lr_refs[z][0, sl, :].astype(BF16), wlr_ref[z, 0]) + blr_ref[z, 0]
                g = _log_sigmoid(logit) * (1.0 / GLA_TAU)
            else:
                qq = _silu(q_refs[z][0, sl, :])
                lb = lb_ref[z]
                f = k_refs[z][0, sl, :]
                ef = jnp.exp(-jnp.abs(f))
                rf = 1.0 / (1.0 + ef)
                ls = jnp.log1p(-lb) + jnp.minimum(f, 0.0) - jnp.log(1.0 + ef)
                llb = jnp.log(lb)
                g = jnp.maximum(llb, ls) + jnp.log(1.0 + jnp.exp(-jnp.abs(llb - ls)))
                kk = (1.0 - lb) * jnp.where(f >= 0.0, ef * rf, rf)
            items.append(dict(z=z, sl=sl, qq=qq, kk=kk, g=g))

    for z in range(2):
        mine = [it_ for it_ in items if it_["z"] == z]
        b3 = _dot(tri_ref[z], jnp.concatenate([_split3(it_["g"]) for it_ in mine], 1))
        for n, it_ in enumerate(mine):
            o3 = 3 * LANES * n
            it_["bc"] = b3[:, o3:o3 + LANES] + b3[:, o3 + LANES:o3 + 2 * LANES] + b3[:, o3 + 2 * LANES:o3 + 3 * LANES]
        bh = _dot(sel_ref[z], jnp.concatenate([it_["bc"] for it_ in mine], 1).astype(BF16))
        for n, it_ in enumerate(mine):
            it_["bh"] = bh[:, n * LANES:(n + 1) * LANES]

    for it_ in items:
        it_["a"] = jnp.where(msk_ref[it_["z"], nlev] > 0.0, _dot_nt(stack(it_["qq"]), it_["kk"].astype(BF16)), 0.0)
    for li, h in enumerate(SCAN_LEVELS):
        upper = (row & (2 * h - 1)) >= h
        for it_ in items:
            z = it_["z"]
            qrow = jnp.logical_not(upper) if z == 1 else upper
            d = it_["bc"] - it_["bh"][li * c:(li + 1) * c]
            e = jnp.exp(jnp.where(qrow, d, -d))
            zz = jnp.where(qrow, it_["qq"], it_["kk"]) * e
            it_["a"] = jnp.where(msk_ref[z, li] > 0.0, _dot_nt(stack(zz), zz.astype(BF16)), it_["a"])

    for it_ in items:
        z = it_["z"]
        ab = it_["a"].astype(BF16)
        vb = v_refs[z][0, it_["sl"], :].astype(BF16)
        it_["o"] = jnp.concatenate(
            [_dot(ab[hd * c:(hd + 1) * c], vb[:, hd * LANES:(hd + 1) * LANES]) for hd in range(nh)], 1)
        bc = it_["bc"]
        blast = bc[0:1] if z == 1 else bc[c - 1:c]
        it_["qin"] = (it_["qq"] * jnp.exp(bc)).astype(BF16)
        upd = _dot_tn(vb, (it_["kk"] * jnp.exp(blast - bc)).astype(BF16))
        if nh == 2:
            r2 = lax.broadcasted_iota(jnp.int32, (2 * LANES, LANES), 0)
            l2 = lax.broadcasted_iota(jnp.int32, (2 * LANES, LANES), 1)
            upd = jnp.where((r2 < LANES) == (l2 < 64), upd, 0.0)
        it_["upd"] = upd
        it_["dec"] = jnp.exp(blast)

    for z in range(2):
        st = st_refs[z][...]
        for it_ in items:
            if it_["z"] != z:
                continue
            o_refs[z][0, it_["sl"], :] = it_["o"] + _dot_nt(it_["qin"], st.astype(BF16))
            st = st * it_["dec"] + it_["upd"]
        st_refs[z][...] = st

    @pl.when(pl.program_id(2) == pl.num_programs(2) - 1)
    def _():
        for z in range(2):
            fin_ref[0, 0, z] = st_refs[z][...]


def _scan_call(mode, p, init, aux):
    b, l, _ = p.shape
    nh = 2 if mode == "gla" else 1
    ninst = 2 if mode == "gla" else HG_HEADS
    t = min(l, 512)
    nb = l // t
    w = LANES * nh

    def col(name, width, z, extra=0):
        base = (OFF[name] + extra) // width
        if z == 0:
            return pl.BlockSpec((1, t, width), lambda bi, n, i: (bi, i, base + n))
        return pl.BlockSpec((1, t, width), lambda bi, n, i: (bi, nb - 1 - i, base + n))

    def fixed(name, z):
        blk = OFF[name] // LANES
        if z == 0:
            return pl.BlockSpec((1, t, LANES), lambda bi, n, i: (bi, i, blk))
        return pl.BlockSpec((1, t, LANES), lambda bi, n, i: (bi, nb - 1 - i, blk))

    if mode == "gla":
        in_specs = [col("a_q", LANES, 0), col("a_q", LANES, 1), col("a_k", LANES, 0), col("a_k", LANES, 1),
                    col("a_v", w, 0), col("a_v", w, 1), fixed("a_lr", 0), fixed("a_lr", 1),
                    pl.BlockSpec((2, 1, LANES, LANES), lambda bi, n, i: (0, n, 0, 0)),
                    pl.BlockSpec((2, 1, 1, LANES), lambda bi, n, i: (0, n, 0, 0))]
        args = [p] * 8 + [aux[0], aux[1]]
    else:
        fw = HG_HEADS * HG_DK
        in_specs = [col("d_q", LANES, 0), col("d_q", LANES, 1), col("d_f", LANES, 0), col("d_f", LANES, 1, fw),
                    col("d_i", LANES, 0), col("d_i", LANES, 1),
                    pl.BlockSpec((2, 1, LANES), lambda bi, n, i: (0, 0, n))]
        args = [p] * 6 + [aux[0]]
    tri, sel, msk = _scan_consts(nh)
    in_specs += [pl.BlockSpec((1, 1, 2, w, LANES), lambda bi, n, i: (bi, n, 0, 0, 0)),
                 pl.BlockSpec(tri.shape, lambda bi, n, i: (0, 0, 0)),
                 pl.BlockSpec(sel.shape, lambda bi, n, i: (0, 0, 0)),
                 pl.BlockSpec(msk.shape, lambda bi, n, i: (0, 0, 0, 0))]
    args += [init, tri, sel, msk]
    return pl.pallas_call(
        functools.partial(_scan_kernel, mode=mode, t=t),
        grid=(b, ninst, nb),
        in_specs=in_specs,
        out_specs=[pl.BlockSpec((1, t, w), lambda bi, n, i: (bi, i, n)),
                   pl.BlockSpec((1, t, w), lambda bi, n, i: (bi, nb - 1 - i, n)),
                   pl.BlockSpec((1, 1, 2, w, LANES), lambda bi, n, i: (bi, n, 0, 0, 0))],
        out_shape=[jax.ShapeDtypeStruct((b, l, ninst * w), F32),
                   jax.ShapeDtypeStruct((b, l, ninst * w), F32),
                   jax.ShapeDtypeStruct((b, ninst, 2, w, LANES), F32)],
        scratch_shapes=[pltpu.VMEM((w, LANES), F32), pltpu.VMEM((w, LANES), F32)],
        compiler_params=_cparams(("parallel", "parallel", "arbitrary")),
        name=f"scan_{mode}",
    )(*args)


def _recurrent_branch(mode, p, pc, aux):
    b = p.shape[0]
    nh = 2 if mode == "gla" else 1
    ninst = 2 if mode == "gla" else HG_HEADS
    zero = jnp.zeros((b, ninst, 2, LANES * nh, LANES), F32)
    oc_f, oc_b, sc = _scan_call(mode, pc, zero, aux)
    o_f, o_b, _ = _scan_call(mode, p, sc, aux)
    return (o_f, o_b), (oc_f, oc_b)


def _na_bias_table(rpb):
    cols = np.arange(GRID_W)
    cs = np.clip(cols - NA_WIN_C // 2, 0, GRID_W - NA_WIN_C)
    kc = np.arange(GRID_W)
    valid = (kc[None, :] >= cs[:, None]) & (kc[None, :] < cs[:, None] + NA_WIN_C)
    col_idx = np.clip(kc[None, :] - cols[:, None] + NA_WIN_C - 1, 0, 2 * NA_WIN_C - 2)
    tiles = jnp.where(valid[None, None], rpb[:, :, col_idx], -jnp.inf)
    nrow = 2 * NA_WIN_R - 1
    tiles = jnp.concatenate([tiles, jnp.full((NA_HEADS, 1, GRID_W, GRID_W), -jnp.inf, F32)], 1)
    rl = np.arange(NA_RB)[:, None]
    kl = np.arange(NA_KR)[None, :]
    spec = ((np.zeros_like(rl), kl - rl + NA_WIN_R - 1),
            (rl, kl - rl + NA_WIN_R // 2 - 1),
            (np.full_like(rl, NA_RB - 1), kl - rl))
    tidx = np.stack([np.where((kl >= lo) & (kl < lo + NA_WIN_R), off, nrow) for lo, off in spec])
    assert tidx.min() >= 0 and tidx.max() <= nrow
    rows = [jnp.concatenate([tiles[:, int(tidx[cls, r, k])] for k in range(NA_KR)], axis=-1)
            for cls in range(3) for r in range(NA_RB)]
    return jnp.stack(rows, 1).reshape(NA_HEADS, 3, NA_RB * GRID_W, NA_KR * GRID_W).astype(F32)


def _softmax_av(parts, extra=None):
    m = functools.reduce(jnp.maximum, [jnp.max(s, axis=-1, keepdims=True) for s, _ in parts])
    if extra is not None:
        m = jnp.maximum(m, extra)
    den = 0.0
    acc = 0.0
    for s, v in parts:
        e = jnp.exp(s - m)
        den = den + jnp.sum(e, axis=-1, keepdims=True)
        acc = acc + _dot(e.astype(BF16), v)
    if extra is not None:
        den = den + jnp.exp(extra - m)
    return acc / den


def _na_kernel(q_ref, k_ref, v_ref, kc_ref, vc_ref, bias_ref, o_ref, *, rows):
    kc = kc_ref[0].astype(BF16)
    vc = vc_ref[0].astype(BF16)
    qn, kn = NA_RB * GRID_W, NA_KR * GRID_W
    nblk = rows // NA_RB
    lane = lax.broadcasted_iota(jnp.int32, (qn, LANES), 1)

    def body(bi, carry):
        r0 = bi * NA_RB
        w0 = jnp.clip(r0 - NA_WIN_R // 2, 0, rows - NA_KR)
        cls = jnp.where(bi == 0, 0, jnp.where(bi == nblk - 1, 2, 1))
        qs = pl.ds(pl.multiple_of(r0 * GRID_W, qn), qn)
        ks = pl.ds(pl.multiple_of(w0 * GRID_W, GRID_W), kn)
        q = q_ref[0, qs, :] * (NA_DH ** -0.5)
        kw = k_ref[0, ks, :].astype(BF16)
        vw = v_ref[0, ks, :].astype(BF16)
        outs = []
        for hd in range(2):
            qm = jnp.where((lane < 64) if hd == 0 else (lane >= 64), q, 0.0).astype(BF16)
            s_loc = _dot_nt(qm, kw) + bias_ref[hd, cls]
            s_ctx = _dot_nt(qm, kc)
            outs.append(_softmax_av([(s_loc, vw), (s_ctx, vc)]))
        o_ref[0, qs, :] = jnp.where(lane < 64, outs[0], outs[1])
        return carry

    lax.fori_loop(0, nblk, body, 0)


def _na_attention(p, pc, bias):
    b, l, _ = p.shape
    lc = pc.shape[1]
    rows = l // GRID_W
    assert rows % NA_RB == 0 and rows >= NA_KR + 1
    npair = NA_HEADS // 2

    def col(name):
        base = OFF[name] // LANES
        return pl.BlockSpec((1, l, LANES), lambda bi, n: (bi, 0, base + n))

    def ccol(name):
        base = OFF[name] // LANES
        return pl.BlockSpec((1, lc, LANES), lambda bi, n: (bi, 0, base + n))

    return pl.pallas_call(
        functools.partial(_na_kernel, rows=rows),
        grid=(b, npair),
        in_specs=[col("b_q"), col("b_k"), col("b_v"), ccol("b_k"), ccol("b_v"),
                  pl.BlockSpec((2, 3, NA_RB * GRID_W, NA_KR * GRID_W), lambda bi, n: (n, 0, 0, 0))],
        out_specs=pl.BlockSpec((1, l, LANES), lambda bi, n: (bi, 0, n)),
        out_shape=jax.ShapeDtypeStruct((b, l, NA_HEADS * NA_DH), F32),
        compiler_params=_cparams(("parallel", "parallel")),
        name="na_attention",
    )(p, p, p, pc, pc, bias)


def _ctx_attn_kernel(q_ref, k_ref, v_ref, sink_ref, o_ref, *, use_sink):
    lc = q_ref.shape[1]
    lane = lax.broadcasted_iota(jnp.int32, (lc, LANES), 1)
    q = q_ref[0] * (NA_DH ** -0.5)
    kb = k_ref[0].astype(BF16)
    vb = v_ref[0].astype(BF16)
    outs = []
    for hd in range(2):
        qm = jnp.where((lane < 64) if hd == 0 else (lane >= 64), q, 0.0).astype(BF16)
        s = _dot_nt(qm, kb)
        extra = sink_ref[0, hd][:, 0:1] if use_sink else None
        outs.append(_softmax_av([(s, vb)], extra))
    o_ref[0] = jnp.where(lane < 64, outs[0], outs[1])


def _ctx_attention(pc, qname, kname, vname, kv_per_q, sink_tab):
    b, lc, _ = pc.shape
    nblk = 4
    use_sink = sink_tab is not None
    if sink_tab is None:
        sink_tab = jnp.zeros((nblk, 2, lc, LANES), F32)
    qb, kb, vb = OFF[qname] // LANES, OFF[kname] // LANES, OFF[vname] // LANES
    kmul = 1 if kv_per_q else 0
    return pl.pallas_call(
        functools.partial(_ctx_attn_kernel, use_sink=use_sink),
        grid=(b, nblk),
        in_specs=[pl.BlockSpec((1, lc, LANES), lambda bi, n: (bi, 0, qb + n)),
                  pl.BlockSpec((1, lc, LANES), lambda bi, n: (bi, 0, kb + kmul * n)),
                  pl.BlockSpec((1, lc, LANES), lambda bi, n: (bi, 0, vb + kmul * n)),
                  pl.BlockSpec((1, 2, lc, LANES), lambda bi, n: (n, 0, 0, 0))],
        out_specs=pl.BlockSpec((1, lc, LANES), lambda bi, n: (bi, 0, n)),
        out_shape=jax.ShapeDtypeStruct((b, lc, nblk * LANES), F32),
        compiler_params=_cparams(("parallel", "parallel")),
        name="ctx_attention",
    )(pc, pc, pc, sink_tab)


def _rope_tables(l):
    nf = WA_DH // 4
    pos = np.arange(l)
    inv = ROPE_BASE ** (-np.arange(nf, dtype=np.float32) / nf)
    d = np.arange(WA_DH)
    p = np.where(d[None, :] < WA_DH // 2, (pos // GRID_W)[:, None], (pos % GRID_W)[:, None]).astype(np.float32)
    ang = p * inv[d % nf][None, :].astype(np.float32)
    cos, sin = np.cos(ang), np.sin(ang)
    first = (d % (2 * nf)) < nf
    ta = np.where(first[None, :], -sin, 0.0)
    tb = np.where(first[None, :], 0.0, sin)
    tile = lambda t: jnp.asarray(np.tile(t, (1, LANES // WA_DH)), F32)
    return tile(cos), tile(ta), tile(tb)


def _rope_kernel(q_ref, k_ref, c_ref, a_ref, b_ref, qo_ref, ko_ref):
    c, a, b = c_ref[...], a_ref[...], b_ref[...]
    nf = WA_DH // 4

    def rot(u):
        return u * c + pltpu.roll(u, LANES - nf, 1) * a + pltpu.roll(u, nf, 1) * b

    for j in range(q_ref.shape[2] // LANES):
        qo_ref[0, :, j * LANES:(j + 1) * LANES] = rot(q_ref[0, :, j * LANES:(j + 1) * LANES])
    ko_ref[0] = rot(k_ref[0])


def _rope(p, tabs):
    b, l, _ = p.shape
    t = min(l, 512)
    qw = WA_HEADS * WA_DH
    tspec = pl.BlockSpec((t, LANES), lambda bi, i: (i, 0))
    return pl.pallas_call(
        _rope_kernel,
        grid=(b, l // t),
        in_specs=[pl.BlockSpec((1, t, qw), lambda bi, i: (bi, i, OFF["c_q"] // qw)),
                  pl.BlockSpec((1, t, LANES), lambda bi, i: (bi, i, OFF["c_k"] // LANES)),
                  tspec, tspec, tspec],
        out_specs=[pl.BlockSpec((1, t, qw), lambda bi, i: (bi, i, 0)),
                   pl.BlockSpec((1, t, LANES), lambda bi, i: (bi, i, 0))],
        out_shape=[jax.ShapeDtypeStruct((b, l, qw), F32), jax.ShapeDtypeStruct((b, l, LANES), F32)],
        compiler_params=_cparams(("parallel", "parallel")),
        name="rope",
    )(p, p, *tabs)


def _wa_kernel(q_ref, k_ref, v_ref, kc_ref, vc_ref, sink_ref, o_ref, *, l):
    n = pl.program_id(1)
    blk = WA_BLOCK
    span = 3 * blk
    start = jnp.clip((n - 1) * blk, 0, l - span)
    ks = pl.ds(pl.multiple_of(start, blk), span)
    kw = k_ref[0, ks, :].astype(BF16)
    vw = v_ref[0, ks, :].astype(BF16)
    kc = kc_ref[0].astype(BF16)
    vc = vc_ref[0].astype(BF16)
    ngrp = WA_HEADS // 2
    lane = lax.broadcasted_iota(jnp.int32, (blk, LANES), 1)
    ii = lax.broadcasted_iota(jnp.int32, (ngrp * blk, span), 0) & (blk - 1)
    jj = lax.broadcasted_iota(jnp.int32, (ngrp * blk, span), 1)
    dist = (jj + start) - (ii + n * blk)
    valid = jnp.abs(dist) <= WA_RADIUS
    halves = []
    for half in range(2):
        keep = (lane < 64) if half == 0 else (lane >= 64)
        qs = jnp.concatenate(
            [jnp.where(keep, q_ref[0, :, j * LANES:(j + 1) * LANES] * (WA_DH ** -0.5), 0.0) for j in range(ngrp)],
            0).astype(BF16)
        s_loc = jnp.where(valid, _dot_nt(qs, kw), -jnp.inf)
        s_ctx = _dot_nt(qs, kc)
        halves.append(_softmax_av([(s_loc, vw), (s_ctx, vc)], sink_ref[half][:, 0:1]))
    for j in range(ngrp):
        o_ref[0, :, j * LANES:(j + 1) * LANES] = jnp.where(lane < 64, halves[0][j * blk:(j + 1) * blk],
                                                           halves[1][j * blk:(j + 1) * blk])


def _wa_attention(qr, kr, p, pc, sink_rows):
    b, l, qw = qr.shape
    lc = pc.shape[1]
    return pl.pallas_call(
        functools.partial(_wa_kernel, l=l),
        grid=(b, l // WA_BLOCK),
        in_specs=[pl.BlockSpec((1, WA_BLOCK, qw), lambda bi, n: (bi, n, 0)),
                  pl.BlockSpec((1, l, LANES), lambda bi, n: (bi, 0, 0)),
                  pl.BlockSpec((1, l, LANES), lambda bi, n: (bi, 0, OFF["c_v"] // LANES)),
                  pl.BlockSpec((1, lc, LANES), lambda bi, n: (bi, 0, OFF["c_k"] // LANES)),
                  pl.BlockSpec((1, lc, LANES), lambda bi, n: (bi, 0, OFF["c_v"] // LANES)),
                  pl.BlockSpec(sink_rows.shape, lambda bi, n: (0, 0, 0))],
        out_specs=pl.BlockSpec((1, WA_BLOCK, qw), lambda bi, n: (bi, n, 0)),
        out_shape=jax.ShapeDtypeStruct((b, l, qw), F32),
        compiler_params=_cparams(("parallel", "arbitrary")),
        name="wa_attention",
    )(qr, kr, p, pc, pc, sink_rows)


def _rms_gate(o, og):
    outs = []
    for hd in range(o.shape[1] // LANES):
        oh = o[:, hd * LANES:(hd + 1) * LANES]
        r = lax.rsqrt(jnp.mean(oh * oh, axis=-1, keepdims=True) + 1e-6)
        outs.append(oh * r * _silu(og[:, hd * LANES:(hd + 1) * LANES]))
    return jnp.concatenate(outs, 1)


def _merge_kernel(oaf_ref, oab_ref, ag_ref, ob_ref, oc_ref, odf_ref, odb_ref, dg_ref, gate_ref, x_ref, mod_ref,
                  wbr_ref, wout_ref, lng_ref, lnb_ref, wr_ref, br_ref, o_ref, g_ref):
    branches = (_rms_gate(oaf_ref[0] + oab_ref[0], ag_ref[0]), ob_ref[0], oc_ref[0],
                _rms_gate(odf_ref[0] + odb_ref[0], dg_ref[0]))
    acc = 0.0
    for i, br in enumerate(branches):
        y = _dot(br.astype(BF16), wbr_ref[i])
        acc = acc + _sigmoid(gate_ref[0, :, i * D_MODEL:(i + 1) * D_MODEL]) * y
    mix = _dot(acc.astype(BF16), wout_ref[...])
    m = mod_ref[0]
    z = ALPHA * x_ref[...] + m[2:3] * mix
    xn = _layer_norm(z, lng_ref[...], lnb_ref[...])
    o_ref[...] = xn
    h = xn * (1.0 + m[4:5]) + m[3:4]
    lane = lax.broadcasted_iota(jnp.int32, (h.shape[0], LANES), 1)
    gl = jnp.where(lane < N_GROUPS, _dot3(h, wr_ref[...]) + br_ref[...], -jnp.inf)
    gmax = jnp.max(gl, axis=-1, keepdims=True)
    g_ref[...] = jnp.min(jnp.where(gl == gmax, lane, LANES), axis=-1, keepdims=True)


def _merge_out_ln(o_a, o_b, o_c, o_d, p, x, mod, wbr, wout, lng, lnb, wr, br):
    b, l, _ = p.shape
    d = x.shape[1]
    tm = min(l, 256)
    nt = l // tm
    gw = N_BRANCH * D_MODEL
    bspec = pl.BlockSpec((1, tm, BRANCH_W), lambda bi, i: (bi, i, 0))

    def pcol(name):
        blk = OFF[name] // BRANCH_W
        return pl.BlockSpec((1, tm, BRANCH_W), lambda bi, i: (bi, i, blk))

    return pl.pallas_call(
        _merge_kernel,
        grid=(b, l // tm),
        in_specs=[bspec, bspec, pcol("a_g"), bspec, bspec, bspec, bspec, pcol("d_g"),
                  pl.BlockSpec((1, tm, gw), lambda bi, i: (bi, i, OFF["gate"] // gw)),
                  pl.BlockSpec((tm, d), lambda bi, i: (bi * nt + i, 0)),
                  pl.BlockSpec((1, 8, d), lambda bi, i: (bi, 0, 0)),
                  pl.BlockSpec((N_BRANCH, BRANCH_W, d), lambda bi, i: (0, 0, 0)),
                  pl.BlockSpec((d, d), lambda bi, i: (0, 0)),
                  pl.BlockSpec((1, d), lambda bi, i: (0, 0)),
                  pl.BlockSpec((1, d), lambda bi, i: (0, 0)),
                  pl.BlockSpec((d, LANES), lambda bi, i: (0, 0)),
                  pl.BlockSpec((1, LANES), lambda bi, i: (0, 0))],
        out_specs=[pl.BlockSpec((tm, d), lambda bi, i: (bi * nt + i, 0)),
                   pl.BlockSpec((tm, 1), lambda bi, i: (bi * nt + i, 0))],
        out_shape=[jax.ShapeDtypeStruct((b * l, d), F32), jax.ShapeDtypeStruct((b * l, 1), jnp.int32)],
        compiler_params=_cparams(("parallel", "parallel")),
        name="merge_out_ln",
    )(o_a[0], o_a[1], p, o_b, o_c, o_d[0], o_d[1], p, p, x, mod, wbr, wout, lng, lnb, wr, br)


def _moe_plan(gidx, n):
    tmax = n // MOE_TM + N_GROUPS
    i32 = jnp.int32
    key = gidx.reshape(n)
    tok = jnp.arange(n, dtype=i32)
    onehot = (key[:, None] == jnp.arange(N_GROUPS, dtype=i32)[None, :]).astype(i32)
    cum = jnp.cumsum(onehot, axis=0)
    counts = cum[-1]
    rank = jnp.sum(onehot * cum, axis=1) - 1
    tiles = (counts + MOE_TM - 1) // MOE_TM
    tile_end = jnp.cumsum(tiles)
    ntiles = tile_end[-1]
    row_start = (tile_end - tiles) * MOE_TM
    ppos = jnp.sum(onehot * row_start[None, :], axis=1) + rank
    placed = jnp.zeros((tmax * MOE_TM,), i32).at[ppos].set(tok).reshape(tmax, MOE_TM)
    last_tok = jnp.max(onehot * tok[:, None], axis=0)
    t = jnp.arange(tmax, dtype=i32)
    tc = jnp.minimum(t, ntiles - 1)
    seg = jnp.minimum(jnp.sum((tile_end[None, :] <= tc[:, None]).astype(i32), axis=1), N_GROUPS - 1)
    in_seg = tc - (tile_end[seg] - tiles[seg])
    off = in_seg[:, None] * MOE_TM + jnp.arange(MOE_TM, dtype=i32)[None, :]
    valid = jnp.logical_and(off < counts[seg][:, None], (t < ntiles)[:, None])
    rows_in = jnp.where(valid, placed, last_tok[seg][:, None])
    rows_out = jnp.where(valid, placed, n + jnp.arange(MOE_TM, dtype=i32)[None, :])
    return seg, rows_in, rows_out


def _moe_tile_kernel(tg_ref, rin_ref, rnext_ref, rout_ref, rvec_ref, x_hbm, mod_ref, wr_ref, br_ref, wgu_ref,
                     wd_ref, lng_ref, lnb_ref, o_hbm, xbuf, ybuf, sem_in, sem_out, *, l):
    t = pl.program_id(0)
    last = pl.num_programs(0) - 1
    tm = MOE_TM
    slot = lax.rem(t, 2)

    def row_in(rows_ref, r, s):
        return pltpu.make_async_copy(x_hbm.at[pl.ds(rows_ref[0, 0, r], 1)], xbuf.at[s, pl.ds(r, 1)], sem_in.at[s])

    def row_out(r):
        return pltpu.make_async_copy(ybuf.at[pl.ds(r, 1)], o_hbm.at[pl.ds(rout_ref[0, 0, r], 1)], sem_out)

    @pl.when(t == 0)
    def _():
        for r in range(tm):
            row_in(rin_ref, r, 0).start()
        ybuf[...] = jnp.zeros_like(ybuf)
        spare = pltpu.make_async_copy(ybuf, o_hbm.at[pl.ds(o_hbm.shape[0] - tm, tm)], sem_out)
        spare.start()
        spare.wait()

    for r in range(tm):
        row_in(rin_ref, r, slot).wait()
    for r in range(tm):
        row_in(rnext_ref, r, 1 - slot).start()

    g = tg_ref[t]
    x = xbuf[slot]
    tok = rvec_ref[0]
    bidx = jnp.zeros((tm, 1), jnp.int32)
    nb = mod_ref.shape[0]
    for k in range(1, nb):
        bidx = bidx + (tok >= k * l).astype(jnp.int32)

    def mod_row(j):
        out = mod_ref[0, j:j + 1, :]
        for k in range(1, nb):
            out = jnp.where(bidx == k, mod_ref[k, j:j + 1, :], out)
        return out

    h = x * (1.0 + mod_row(4)) + mod_row(3)
    lane = lax.broadcasted_iota(jnp.int32, (tm, LANES), 1)
    ninf = -jnp.inf
    lg = _dot3(h, wr_ref[...]) + br_ref[...]
    gl = jnp.where(lane < N_GROUPS, lg, ninf)
    gmax = jnp.max(gl, axis=-1, keepdims=True)
    gsel = jnp.sum(jnp.where(lane == g, lg, 0.0), axis=-1, keepdims=True)
    gw = jnp.exp(gsel - gmax) / jnp.sum(jnp.exp(gl - gmax), axis=-1, keepdims=True)
    e0 = N_GROUPS + g * EXPERTS_PER_GROUP
    ingrp = jnp.logical_and(lane >= e0, lane < e0 + EXPERTS_PER_GROUP)
    e1 = jnp.where(ingrp, lg, ninf)
    m1 = jnp.max(e1, axis=-1, keepdims=True)
    i1 = jnp.min(jnp.where(e1 == m1, lane, LANES), axis=-1, keepdims=True)
    e2 = jnp.where(lane == i1, ninf, e1)
    m2 = jnp.max(e2, axis=-1, keepdims=True)
    i2 = jnp.min(jnp.where(e2 == m2, lane, LANES), axis=-1, keepdims=True)
    tt = jnp.exp(m2 - m1)
    wts = gw * (jnp.where(lane == i1, 1.0 / (1.0 + tt), 0.0) + jnp.where(lane == i2, tt / (1.0 + tt), 0.0))

    hb = h.astype(BF16)
    acc = jnp.zeros((tm, D_MODEL), F32)
    for j in range(EXPERTS_PER_GROUP):
        wcol = jnp.sum(jnp.where(lane == e0 + j, wts, 0.0), axis=-1, keepdims=True)
        gu = _dot(hb, wgu_ref[j])
        hid = _silu(gu[:, :EXPERT_HIDDEN]) * gu[:, EXPERT_HIDDEN:] * wcol
        acc = acc + _dot(hid.astype(BF16), wd_ref[j])
    y = _layer_norm(ALPHA * x + mod_row(5) * acc, lng_ref[...], lnb_ref[...])

    @pl.when(t > 0)
    def _():
        for r in range(tm):
            row_out(r).wait()

    ybuf[...] = y
    for r in range(tm):
        row_out(r).start()

    @pl.when(t == last)
    def _():
        for r in range(tm):
            row_out(r).wait()
        for r in range(tm):
            row_in(rnext_ref, r, 1 - slot).wait()


def _moe_ln(x, gidx, b, l, mod, wr, br, wgu, wd, lng, lnb):
    d = x.shape[1]
    n = b * l
    tile_group, rows_in, rows_out = _moe_plan(gidx, n)
    tmax = rows_in.shape[0]
    cst = lambda shape: pl.BlockSpec(shape, lambda t, tg: tuple(0 for _ in shape))
    smem_rows = lambda fn: pl.BlockSpec((1, 1, MOE_TM), fn, memory_space=pltpu.SMEM)
    grid_spec = pltpu.PrefetchScalarGridSpec(
        num_scalar_prefetch=1,
        grid=(tmax,),
        in_specs=[smem_rows(lambda t, tg: (t, 0, 0)),
                  smem_rows(lambda t, tg: (jnp.minimum(t + 1, tmax - 1), 0, 0)),
                  smem_rows(lambda t, tg: (t, 0, 0)),
                  pl.BlockSpec((1, MOE_TM, 1), lambda t, tg: (t, 0, 0)),
                  pl.BlockSpec(memory_space=pl.ANY),
                  cst(mod.shape), cst((d, LANES)), cst((1, LANES)),
                  pl.BlockSpec((EXPERTS_PER_GROUP, d, 2 * EXPERT_HIDDEN), lambda t, tg: (tg[t], 0, 0)),
                  pl.BlockSpec((EXPERTS_PER_GROUP, EXPERT_HIDDEN, d), lambda t, tg: (tg[t], 0, 0)),
                  cst((1, d)), cst((1, d))],
        out_specs=pl.BlockSpec(memory_space=pl.ANY),
        scratch_shapes=[pltpu.VMEM((2, MOE_TM, d), F32), pltpu.VMEM((MOE_TM, d), F32),
                        pltpu.SemaphoreType.DMA((2,)), pltpu.SemaphoreType.DMA],
    )
    rin3 = rows_in.reshape(tmax, 1, MOE_TM)
    return pl.pallas_call(
        functools.partial(_moe_tile_kernel, l=l),
        grid_spec=grid_spec,
        out_shape=jax.ShapeDtypeStruct((n + MOE_TM, d), F32),
        compiler_params=_cparams(("arbitrary",)),
        name="moe_ln",
    )(tile_group, rin3, rin3, rows_out.reshape(tmax, 1, MOE_TM), rows_in.reshape(tmax, MOE_TM, 1), x, mod, wr, br,
      wgu, wd, lng, lnb)


def _pad_cols(w, n):
    return jnp.pad(w, ((0, 0), (0, n - w.shape[1])))


def _wa_head_pairs(w, axis):
    shp = w.shape
    w = w.reshape(shp[:axis] + (2, WA_HEADS // 2, WA_DH) + shp[axis + 1:])
    w = jnp.swapaxes(w, axis, axis + 1)
    return w.reshape(shp)


def _prep_w_in(w):
    segs = []
    for name in _NEW_ORDER:
        s = w[:, _ORIG_OFF[name]:_ORIG_OFF[name] + _SIZES[name]]
        if name == "c_q":
            s = _wa_head_pairs(s, 1)
        segs.append(s)
    segs.append(jnp.zeros((w.shape[0], NP - D_IN), w.dtype))
    return jnp.concatenate(segs, 1).astype(BF16)


def _prep_gla_gate(lr_w, lr_b):
    w = jnp.zeros((2, 2, LANES, LANES), F32)
    for z in range(2):
        for n in range(2):
            w = w.at[z, n, z * GLA_RANK:(z + 1) * GLA_RANK, :].set(lr_w[z][:, n * LANES:(n + 1) * LANES])
    return w.astype(BF16), lr_b.reshape(2, 2, 1, LANES)


def _sink_rows(sink, rows_per_head):
    s = sink.reshape(2, 4)
    return jnp.broadcast_to(s[:, :, None, None], (2, 4, rows_per_head, LANES)).reshape(2, 4 * rows_per_head, LANES)


def kernel(x, c, ctx, c_ctx, w_mod, b_mod, w_in, gla_lr_w, gla_lr_b, hg_gamma, na_rpb, wa_sink, w_branch, w_out,
           ln_g, ln_b, moe_w_group, moe_b_group, moe_w_expert, moe_b_expert, moe_w_gate, moe_w_up, moe_w_down):
    b, l, d = x.shape
    lc = ctx.shape[1]
    assert d == D_MODEL and b + 1 <= 8 and l % 512 == 0 and l >= 3 * WA_BLOCK and lc % SCAN_C == 0

    cc = jnp.concatenate([c, c_ctx[None], jnp.zeros((8 - b - 1, d), F32)], 0)
    mods = _modulation(cc, w_mod, b_mod)
    cum = jnp.cumsum(jax.nn.softmax(hg_gamma.astype(F32), axis=1), axis=1)
    lower_bounds = cum - cum[:, :1]
    rope_tabs = _rope_tables(l)

    x = x.reshape(b * l, d)
    cx = ctx.reshape(b * lc, d)
    for li in range(DEPTH):
        with_ctx = li < DEPTH - 1
        mod6 = mods[li].reshape(8, 6, d)
        mod = jnp.pad(mod6[:b], ((0, 0), (0, 2), (0, 0)))
        modc = jnp.broadcast_to(jnp.pad(mod6[b], ((0, 2), (0, 0)))[None], (b, 8, d))
        w_in_p = _prep_w_in(w_in[li])
        p = _in_proj(x, b, l, mod, w_in_p)
        pc = _in_proj(cx, b, lc, modc, w_in_p)

        gla_aux = _prep_gla_gate(gla_lr_w[li], gla_lr_b[li])
        o_a, oc_a = _recurrent_branch("gla", p, pc, gla_aux)
        hg_aux = (lower_bounds[:, li].reshape(2, 1, HG_HEADS * HG_DK),)
        o_d, oc_d = _recurrent_branch("hgrn", p, pc, hg_aux)

        o_b = _na_attention(p, pc, _na_bias_table(na_rpb[li]))
        qr, kr = _rope(p, rope_tabs)
        o_c = _wa_attention(qr, kr, p, pc, _sink_rows(wa_sink[li], WA_BLOCK))

        wbr = jnp.stack([w_branch[li, 0], w_branch[li, 1], _wa_head_pairs(w_branch[li, 2], 0),
                         w_branch[li, 3]]).astype(BF16)
        wout = w_out[li].astype(BF16)
        lng0, lnb0 = ln_g[li, 0][None], ln_b[li, 0][None]
        wr = _pad_cols(jnp.concatenate([moe_w_group[li], moe_w_expert[li]], 1), LANES)
        br = _pad_cols(jnp.concatenate([moe_b_group[li], moe_b_expert[li]])[None], LANES)
        x, gidx = _merge_out_ln(o_a, o_b, o_c, o_d, p, x, mod, wbr, wout, lng0, lnb0, wr, br)

        wgu = jnp.concatenate([moe_w_gate[li].astype(BF16), moe_w_up[li].astype(BF16)], -1)
        wgu = wgu.reshape(N_EXPERTS, d, 2 * EXPERT_HIDDEN)
        wd = moe_w_down[li].reshape(N_EXPERTS, EXPERT_HIDDEN, d).astype(BF16)
        lng1, lnb1 = ln_g[li, 1][None], ln_b[li, 1][None]
        moe = functools.partial(_moe_ln, wr=wr, br=br, wgu=wgu, wd=wd, lng=lng1, lnb=lnb1)
        if with_ctx:
            oc_b = _ctx_attention(pc, "b_q", "b_k", "b_v", True, None)
            oc_c = _ctx_attention(pc, "c_q", "c_k", "c_v", False,
                                  jnp.broadcast_to(wa_sink[li].reshape(2, 4).T[:, :, None, None], (4, 2, lc, LANES)))
            cx, gidx_c = _merge_out_ln(oc_a, oc_b, oc_c, oc_d, pc, cx, modc, wbr, wout, lng0, lnb0, wr, br)
            cx = moe(cx, gidx_c, b, lc, modc)
        x = moe(x, gidx, b, l, mod)
    return x[:b * l].reshape(b, l, d)
```

```python
import functools

import numpy as np
import jax
import jax.numpy as jnp
from jax import lax
from jax.experimental import pallas as pl
from jax.experimental.pallas import tpu as pltpu

F32 = jnp.float32
BF16 = jnp.bfloat16
HIGHEST = lax.Precision.HIGHEST

LANES = 128
VMEM_LIMIT = 56 * 1024 * 1024

D_MODEL = 1024
DEPTH = 2
GRID_W = 64
GLA_HEADS, GLA_DK, GLA_DV, GLA_RANK, GLA_TAU = 4, 64, 128, 16, 16.0
NA_HEADS, NA_DH, NA_WIN_R, NA_WIN_C = 8, 64, 8, 16
WA_HEADS, WA_KV_HEADS, WA_DH, WA_RADIUS, WA_BLOCK = 8, 2, 64, 128, 128
ROPE_BASE = 10000.0
HG_HEADS, HG_DK, HG_DV = 4, 128, 128
N_BRANCH, BRANCH_W = 4, 512
N_GROUPS, EXPERTS_PER_GROUP, EXPERT_HIDDEN = 4, 8, 256
N_EXPERTS = N_GROUPS * EXPERTS_PER_GROUP
LN_EPS = 1e-5
ALPHA = (2.0 * DEPTH) ** 0.25

_IN_COLS = (
    ("a_q", 256), ("a_k", 256), ("a_v", 512), ("a_g", 512), ("a_lr", 32),
    ("b_q", 512), ("b_k", 512), ("b_v", 512),
    ("c_q", 512), ("c_k", 128), ("c_v", 128),
    ("d_q", 512), ("d_f", 1024), ("d_i", 512), ("d_g", 512), ("gate", 4096),
)
_ORIG_OFF = {}
_o = 0
for _n, _s in _IN_COLS:
    _ORIG_OFF[_n] = _o
    _o += _s
D_IN = _o
_NEW_ORDER = ("gate", "a_q", "a_k", "a_v", "a_g", "b_q", "b_k", "b_v", "c_q",
              "d_q", "d_f", "d_i", "d_g", "c_k", "c_v", "a_lr")
_SIZES = dict(_IN_COLS)
OFF = {}
_o = 0
for _n in _NEW_ORDER:
    OFF[_n] = _o
    _o += _SIZES[_n]
PROJ_TN = 768
NP = ((_o + PROJ_TN - 1) // PROJ_TN) * PROJ_TN

SCAN_C = 128
SCAN_LEVELS = (64, 32, 16, 8, 4, 2, 1)

MOE_TM = 256

NA_RB = 4
NA_KR = NA_RB + NA_WIN_R - 1


def _cparams(sem, vmem=VMEM_LIMIT):
    return pltpu.CompilerParams(dimension_semantics=sem, vmem_limit_bytes=vmem)


def _dot(a, b, **kw):
    return jnp.dot(a, b, preferred_element_type=F32, **kw)


def _dot_nt(a, b):
    return lax.dot_general(a, b, (((1,), (1,)), ((), ())), preferred_element_type=F32)


def _dot_tn(a, b):
    return lax.dot_general(a, b, (((0,), (0,)), ((), ())), preferred_element_type=F32)


def _dot3(a, w):
    a1 = a.astype(BF16)
    a2 = (a - a1.astype(F32)).astype(BF16)
    w1 = w.astype(BF16)
    w2 = (w - w1.astype(F32)).astype(BF16)
    return _dot(a1, w1) + (_dot(a1, w2) + _dot(a2, w1))


def _sigmoid(x):
    return 1.0 / (1.0 + jnp.exp(-x))


def _silu(x):
    return x * _sigmoid(x)


def _log_sigmoid(x):
    return jnp.minimum(x, 0.0) - jnp.log(1.0 + jnp.exp(-jnp.abs(x)))


def _layer_norm(z, g, b):
    mu = jnp.mean(z, axis=-1, keepdims=True)
    zc = z - mu
    var = jnp.mean(zc * zc, axis=-1, keepdims=True)
    return zc * lax.rsqrt(var + LN_EPS) * g + b


def _mod_kernel(c_ref, w_ref, b_ref, o_ref):
    c = c_ref[...]
    o_ref[0] = _dot(_silu(c), w_ref[0], precision=HIGHEST) + b_ref[0]


def _modulation(cc, w_mod, b_mod):
    depth, d, n = w_mod.shape
    tn = 1536
    return pl.pallas_call(
        _mod_kernel,
        grid=(depth, n // tn),
        in_specs=[pl.BlockSpec((8, d), lambda l, j: (0, 0)),
                  pl.BlockSpec((1, d, tn), lambda l, j: (l, 0, j)),
                  pl.BlockSpec((1, 1, tn), lambda l, j: (l, 0, j))],
        out_specs=pl.BlockSpec((1, 8, tn), lambda l, j: (l, 0, j)),
        out_shape=jax.ShapeDtypeStruct((depth, 8, n), F32),
        compiler_params=_cparams(("parallel", "parallel")),
        name="modulation",
    )(cc, w_mod, b_mod.reshape(depth, 1, n))


def _in_proj_kernel(x_ref, mod_ref, w_ref, o_ref, h_ref):
    @pl.when(pl.program_id(2) == 0)
    def _():
        m = mod_ref[0]
        h_ref[...] = (x_ref[...] * (1.0 + m[1:2]) + m[0:1]).astype(BF16)

    o_ref[0] = _dot(h_ref[...], w_ref[...])


def _in_proj(x, b, l, mod, w):
    d = x.shape[1]
    tm = min(l, 2048)
    nt = l // tm
    return pl.pallas_call(
        _in_proj_kernel,
        grid=(b, nt, NP // PROJ_TN),
        in_specs=[pl.BlockSpec((tm, d), lambda bi, i, j: (bi * nt + i, 0)),
                  pl.BlockSpec((1, 8, d), lambda bi, i, j: (bi, 0, 0)),
                  pl.BlockSpec((d, PROJ_TN), lambda bi, i, j: (0, j))],
        out_specs=pl.BlockSpec((1, tm, PROJ_TN), lambda bi, i, j: (bi, i, j)),
        out_shape=jax.ShapeDtypeStruct((b, l, NP), F32),
        scratch_shapes=[pltpu.VMEM((tm, d), BF16)],
        compiler_params=_cparams(("parallel", "parallel", "arbitrary")),
        name="in_proj",
    )(x, mod, w)


def _scan_consts(nh):
    c = SCAN_C
    i = np.arange(c)[:, None]
    j = np.arange(c)[None, :]
    tris, sels, msks = [], [], []
    for reverse in (False, True):
        tris.append((j >= i) if reverse else (j <= i))
        sel, masks = [], []
        for h in SCAN_LEVELS:
            same = (i // (2 * h)) == (j // (2 * h))
            base = (np.arange(c) // (2 * h)) * (2 * h)
            if reverse:
                pos = base + h
                m = same & ((i % (2 * h)) < h) & ((j % (2 * h)) >= h)
            else:
                pos = base + h - 1
                m = same & ((i % (2 * h)) >= h) & ((j % (2 * h)) < h)
            sel.append(j == pos[:, None])
            masks.append(m)
        masks.append(i == j)
        sels.append(np.concatenate(sel, 0))
        msks.append(np.stack([np.tile(m, (nh, 1)) for m in masks]))
    return (jnp.asarray(np.stack(tris), BF16), jnp.asarray(np.stack(sels), BF16),
            jnp.asarray(np.stack(msks), F32))


def _split3(x):
    a = x.astype(BF16)
    r = x - a.astype(F32)
    b = r.astype(BF16)
    c = (r - b.astype(F32)).astype(BF16)
    return jnp.concatenate([a, b, c], 1)


def _scan_kernel(*refs, mode, t):
    nh = 2 if mode == "gla" else 1
    it = iter(refs)
    q_refs, k_refs, v_refs = (next(it), next(it)), (next(it), next(it)), (next(it), next(it))
    if mode == "gla":
        lr_refs = (next(it), next(it))
        wlr_ref, blr_ref = next(it), next(it)
    else:
        lb_ref = next(it)
    init_ref, tri_ref, sel_ref, msk_ref = next(it), next(it), next(it), next(it)
    o_refs = (next(it), next(it))
    fin_ref = next(it)
    st_refs = (next(it), next(it))

    @pl.when(pl.program_id(2) == 0)
    def _():
        for z in range(2):
            st_refs[z][...] = init_ref[0, 0, z]

    c = SCAN_C
    nc = t // c
    nlev = len(SCAN_LEVELS)
    row = lax.broadcasted_iota(jnp.int32, (c, LANES), 0)
    lane = lax.broadcasted_iota(jnp.int32, (c, LANES), 1)

    def stack(x):
        if nh == 1:
            return x.astype(BF16)
        return jnp.concatenate([jnp.where(lane < 64, x, 0.0), jnp.where(lane >= 64, x, 0.0)], 0).astype(BF16)

    order = [[ci if z == 0 else nc - 1 - ci for ci in range(nc)] for z in range(2)]
    items = []
    for z in range(2):
        for cc in order[z]:
            sl = pl.ds(cc * c, c)
            if mode == "gla":
                qq = q_refs[z][0, sl, :] * (GLA_DK ** -0.5)
                kk = k_refs[z][0, sl, :]
                logit = _dot(lr_refs[z][0, sl, :].astype(BF16), wlr_ref[z, 0]) + blr_ref[z, 0]
                g = _log_sigmoid(logit) * (1.0 / GLA_TAU)
            else:
                qq = _silu(q_refs[z][0, sl, :])
                lb = lb_ref[z]
                f = k_refs[z][0, sl, :]
                ef = jnp.exp(-jnp.abs(f))
                rf = 1.0 / (1.0 + ef)
                ls = jnp.log1p(-lb) + jnp.minimum(f, 0.0) - jnp.log(1.0 + ef)
                llb = jnp.log(lb)
                g = jnp.maximum(llb, ls) + jnp.log(1.0 + jnp.exp(-jnp.abs(llb - ls)))
                kk = (1.0 - lb) * jnp.where(f >= 0.0, ef * rf, rf)
            items.append(dict(z=z, sl=sl, qq=qq, kk=kk, g=g))

    for z in range(2):
        mine = [it_ for it_ in items if it_["z"] == z]
        b3 = _dot(tri_ref[z], jnp.concatenate([_split3(it_["g"]) for it_ in mine], 1))
        for n, it_ in enumerate(mine):
            o3 = 3 * LANES * n
            it_["bc"] = b3[:, o3:o3 + LANES] + b3[:, o3 + LANES:o3 + 2 * LANES] + b3[:, o3 + 2 * LANES:o3 + 3 * LANES]
        bh = _dot(sel_ref[z], jnp.concatenate([it_["bc"] for it_ in mine], 1).astype(BF16))
        for n, it_ in enumerate(mine):
            it_["bh"] = bh[:, n * LANES:(n + 1) * LANES]

    for it_ in items:
        it_["a"] = jnp.where(msk_ref[it_["z"], nlev] > 0.0, _dot_nt(stack(it_["qq"]), it_["kk"].astype(BF16)), 0.0)
    for li, h in enumerate(SCAN_LEVELS):
        upper = (row & (2 * h - 1)) >= h
        for it_ in items:
            z = it_["z"]
            qrow = jnp.logical_not(upper) if z == 1 else upper
            d = it_["bc"] - it_["bh"][li * c:(li + 1) * c]
            e = jnp.exp(jnp.where(qrow, d, -d))
            zz = jnp.where(qrow, it_["qq"], it_["kk"]) * e
            it_["a"] = jnp.where(msk_ref[z, li] > 0.0, _dot_nt(stack(zz), zz.astype(BF16)), it_["a"])

    for it_ in items:
        z = it_["z"]
        ab = it_["a"].astype(BF16)
        vb = v_refs[z][0, it_["sl"], :].astype(BF16)
        it_["o"] = jnp.concatenate(
            [_dot(ab[hd * c:(hd + 1) * c], vb[:, hd * LANES:(hd + 1) * LANES]) for hd in range(nh)], 1)
        bc = it_["bc"]
        blast = bc[0:1] if z == 1 else bc[c - 1:c]
        it_["qin"] = (it_["qq"] * jnp.exp(bc)).astype(BF16)
        upd = _dot_tn(vb, (it_["kk"] * jnp.exp(blast - bc)).astype(BF16))
        if nh == 2:
            r2 = lax.broadcasted_iota(jnp.int32, (2 * LANES, LANES), 0)
            l2 = lax.broadcasted_iota(jnp.int32, (2 * LANES, LANES), 1)
            upd = jnp.where((r2 < LANES) == (l2 < 64), upd, 0.0)
        it_["upd"] = upd
        it_["dec"] = jnp.exp(blast)

    for z in range(2):
        st = st_refs[z][...]
        for it_ in items:
            if it_["z"] != z:
                continue
            o_refs[z][0, it_["sl"], :] = it_["o"] + _dot_nt(it_["qin"], st.astype(BF16))
            st = st * it_["dec"] + it_["upd"]
        st_refs[z][...] = st

    @pl.when(pl.program_id(2) == pl.num_programs(2) - 1)
    def _():
        for z in range(2):
            fin_ref[0, 0, z] = st_refs[z][...]


def _scan_call(mode, p, init, aux):
    b, l, _ = p.shape
    nh = 2 if mode == "gla" else 1
    ninst = 2 if mode == "gla" else HG_HEADS
    t = min(l, 512)
    nb = l // t
    w = LANES * nh

    def col(name, width, z, extra=0):
        base = (OFF[name] + extra) // width
        if z == 0:
            return pl.BlockSpec((1, t, width), lambda bi, n, i: (bi, i, base + n))
        return pl.BlockSpec((1, t, width), lambda bi, n, i: (bi, nb - 1 - i, base + n))

    def fixed(name, z):
        blk = OFF[name] // LANES
        if z == 0:
            return pl.BlockSpec((1, t, LANES), lambda bi, n, i: (bi, i, blk))
        return pl.BlockSpec((1, t, LANES), lambda bi, n, i: (bi, nb - 1 - i, blk))

    if mode == "gla":
        in_specs = [col("a_q", LANES, 0), col("a_q", LANES, 1), col("a_k", LANES, 0), col("a_k", LANES, 1),
                    col("a_v", w, 0), col("a_v", w, 1), fixed("a_lr", 0), fixed("a_lr", 1),
                    pl.BlockSpec((2, 1, LANES, LANES), lambda bi, n, i: (0, n, 0, 0)),
                    pl.BlockSpec((2, 1, 1, LANES), lambda bi, n, i: (0, n, 0, 0))]
        args = [p] * 8 + [aux[0], aux[1]]
    else:
        fw = HG_HEADS * HG_DK
        in_specs = [col("d_q", LANES, 0), col("d_q", LANES, 1), col("d_f", LANES, 0), col("d_f", LANES, 1, fw),
                    col("d_i", LANES, 0), col("d_i", LANES, 1),
                    pl.BlockSpec((2, 1, LANES), lambda bi, n, i: (0, 0, n))]
        args = [p] * 6 + [aux[0]]
    tri, sel, msk = _scan_consts(nh)
    in_specs += [pl.BlockSpec((1, 1, 2, w, LANES), lambda bi, n, i: (bi, n, 0, 0, 0)),
                 pl.BlockSpec(tri.shape, lambda bi, n, i: (0, 0, 0)),
                 pl.BlockSpec(sel.shape, lambda bi, n, i: (0, 0, 0)),
                 pl.BlockSpec(msk.shape, lambda bi, n, i: (0, 0, 0, 0))]
    args += [init, tri, sel, msk]
    return pl.pallas_call(
        functools.partial(_scan_kernel, mode=mode, t=t),
        grid=(b, ninst, nb),
        in_specs=in_specs,
        out_specs=[pl.BlockSpec((1, t, w), lambda bi, n, i: (bi, i, n)),
                   pl.BlockSpec((1, t, w), lambda bi, n, i: (bi, nb - 1 - i, n)),
                   pl.BlockSpec((1, 1, 2, w, LANES), lambda bi, n, i: (bi, n, 0, 0, 0))],
        out_shape=[jax.ShapeDtypeStruct((b, l, ninst * w), F32),
                   jax.ShapeDtypeStruct((b, l, ninst * w), F32),
                   jax.ShapeDtypeStruct((b, ninst, 2, w, LANES), F32)],
        scratch_shapes=[pltpu.VMEM((w, LANES), F32), pltpu.VMEM((w, LANES), F32)],
        compiler_params=_cparams(("parallel", "parallel", "arbitrary")),
        name=f"scan_{mode}",
    )(*args)


def _recurrent_branch(mode, p, pc, aux):
    b = p.shape[0]
    nh = 2 if mode == "gla" else 1
    ninst = 2 if mode == "gla" else HG_HEADS
    zero = jnp.zeros((b, ninst, 2, LANES * nh, LANES), F32)
    oc_f, oc_b, sc = _scan_call(mode, pc, zero, aux)
    o_f, o_b, _ = _scan_call(mode, p, sc, aux)
    return (o_f, o_b), (oc_f, oc_b)


def _na_bias_table(rpb):
    cols = np.arange(GRID_W)
    cs = np.clip(cols - NA_WIN_C // 2, 0, GRID_W - NA_WIN_C)
    kc = np.arange(GRID_W)
    valid = (kc[None, :] >= cs[:, None]) & (kc[None, :] < cs[:, None] + NA_WIN_C)
    col_idx = np.clip(kc[None, :] - cols[:, None] + NA_WIN_C - 1, 0, 2 * NA_WIN_C - 2)
    tiles = jnp.where(valid[None, None], rpb[:, :, col_idx], -jnp.inf)
    nrow = 2 * NA_WIN_R - 1
    tiles = jnp.concatenate([tiles, jnp.full((NA_HEADS, 1, GRID_W, GRID_W), -jnp.inf, F32)], 1)
    rl = np.arange(NA_RB)[:, None]
    kl = np.arange(NA_KR)[None, :]
    spec = ((np.zeros_like(rl), kl - rl + NA_WIN_R - 1),
            (rl, kl - rl + NA_WIN_R // 2 - 1),
            (np.full_like(rl, NA_RB - 1), kl - rl))
    tidx = np.stack([np.where((kl >= lo) & (kl < lo + NA_WIN_R), off, nrow) for lo, off in spec])
    assert tidx.min() >= 0 and tidx.max() <= nrow
    rows = [jnp.concatenate([tiles[:, int(tidx[cls, r, k])] for k in range(NA_KR)], axis=-1)
            for cls in range(3) for r in range(NA_RB)]
    return jnp.stack(rows, 1).reshape(NA_HEADS, 3, NA_RB * GRID_W, NA_KR * GRID_W).astype(F32)


def _softmax_av(parts, extra=None):
    m = functools.reduce(jnp.maximum, [jnp.max(s, axis=-1, keepdims=True) for s, _ in parts])
    if extra is not None:
        m = jnp.maximum(m, extra)
    den = 0.0
    acc = 0.0
    for s, v in parts:
        e = jnp.exp(s - m)
        den = den + jnp.sum(e, axis=-1, keepdims=True)
        acc = acc + _dot(e.astype(BF16), v)
    if extra is not None:
        den = den + jnp.exp(extra - m)
    return acc / den


def _na_kernel(q_ref, k_ref, v_ref, kc_ref, vc_ref, bias_ref, o_ref, *, rows):
    kc = kc_ref[0].astype(BF16)
    vc = vc_ref[0].astype(BF16)
    qn, kn = NA_RB * GRID_W, NA_KR * GRID_W
    nblk = rows // NA_RB
    lane = lax.broadcasted_iota(jnp.int32, (qn, LANES), 1)

    def body(bi, carry):
        r0 = bi * NA_RB
        w0 = jnp.clip(r0 - NA_WIN_R // 2, 0, rows - NA_KR)
        cls = jnp.where(bi == 0, 0, jnp.where(bi == nblk - 1, 2, 1))
        qs = pl.ds(pl.multiple_of(r0 * GRID_W, qn), qn)
        ks = pl.ds(pl.multiple_of(w0 * GRID_W, GRID_W), kn)
        q = q_ref[0, qs, :] * (NA_DH ** -0.5)
        kw = k_ref[0, ks, :].astype(BF16)
        vw = v_ref[0, ks, :].astype(BF16)
        outs = []
        for hd in range(2):
            qm = jnp.where((lane < 64) if hd == 0 else (lane >= 64), q, 0.0).astype(BF16)
            s_loc = _dot_nt(qm, kw) + bias_ref[hd, cls]
            s_ctx = _dot_nt(qm, kc)
            outs.append(_softmax_av([(s_loc, vw), (s_ctx, vc)]))
        o_ref[0, qs, :] = jnp.where(lane < 64, outs[0], outs[1])
        return carry

    lax.fori_loop(0, nblk, body, 0)


def _na_attention(p, pc, bias):
    b, l, _ = p.shape
    lc = pc.shape[1]
    rows = l // GRID_W
    assert rows % NA_RB == 0 and rows >= NA_KR + 1
    npair = NA_HEADS // 2

    def col(name):
        base = OFF[name] // LANES
        return pl.BlockSpec((1, l, LANES), lambda bi, n: (bi, 0, base + n))

    def ccol(name):
        base = OFF[name] // LANES
        return pl.BlockSpec((1, lc, LANES), lambda bi, n: (bi, 0, base + n))

    return pl.pallas_call(
        functools.partial(_na_kernel, rows=rows),
        grid=(b, npair),
        in_specs=[col("b_q"), col("b_k"), col("b_v"), ccol("b_k"), ccol("b_v"),
                  pl.BlockSpec((2, 3, NA_RB * GRID_W, NA_KR * GRID_W), lambda bi, n: (n, 0, 0, 0))],
        out_specs=pl.BlockSpec((1, l, LANES), lambda bi, n: (bi, 0, n)),
        out_shape=jax.ShapeDtypeStruct((b, l, NA_HEADS * NA_DH), F32),
        compiler_params=_cparams(("parallel", "parallel")),
        name="na_attention",
    )(p, p, p, pc, pc, bias)


def _ctx_attn_kernel(q_ref, k_ref, v_ref, sink_ref, o_ref, *, use_sink):
    lc = q_ref.shape[1]
    lane = lax.broadcasted_iota(jnp.int32, (lc, LANES), 1)
    q = q_ref[0] * (NA_DH ** -0.5)
    kb = k_ref[0].astype(BF16)
    vb = v_ref[0].astype(BF16)
    outs = []
    for hd in range(2):
        qm = jnp.where((lane < 64) if hd == 0 else (lane >= 64), q, 0.0).astype(BF16)
        s = _dot_nt(qm, kb)
        extra = sink_ref[0, hd][:, 0:1] if use_sink else None
        outs.append(_softmax_av([(s, vb)], extra))
    o_ref[0] = jnp.where(lane < 64, outs[0], outs[1])


def _ctx_attention(pc, qname, kname, vname, kv_per_q, sink_tab):
    b, lc, _ = pc.shape
    nblk = 4
    use_sink = sink_tab is not None
    if sink_tab is None:
        sink_tab = jnp.zeros((nblk, 2, lc, LANES), F32)
    qb, kb, vb = OFF[qname] // LANES, OFF[kname] // LANES, OFF[vname] // LANES
    kmul = 1 if kv_per_q else 0
    return pl.pallas_call(
        functools.partial(_ctx_attn_kernel, use_sink=use_sink),
        grid=(b, nblk),
        in_specs=[pl.BlockSpec((1, lc, LANES), lambda bi, n: (bi, 0, qb + n)),
                  pl.BlockSpec((1, lc, LANES), lambda bi, n: (bi, 0, kb + kmul * n)),
                  pl.BlockSpec((1, lc, LANES), lambda bi, n: (bi, 0, vb + kmul * n)),
                  pl.BlockSpec((1, 2, lc, LANES), lambda bi, n: (n, 0, 0, 0))],
        out_specs=pl.BlockSpec((1, lc, LANES), lambda bi, n: (bi, 0, n)),
        out_shape=jax.ShapeDtypeStruct((b, lc, nblk * LANES), F32),
        compiler_params=_cparams(("parallel", "parallel")),
        name="ctx_attention",
    )(pc, pc, pc, sink_tab)


def _rope_tables(l):
    nf = WA_DH // 4
    pos = np.arange(l)
    inv = ROPE_BASE ** (-np.arange(nf, dtype=np.float32) / nf)
    d = np.arange(WA_DH)
    p = np.where(d[None, :] < WA_DH // 2, (pos // GRID_W)[:, None], (pos % GRID_W)[:, None]).astype(np.float32)
    ang = p * inv[d % nf][None, :].astype(np.float32)
    cos, sin = np.cos(ang), np.sin(ang)
    first = (d % (2 * nf)) < nf
    ta = np.where(first[None, :], -sin, 0.0)
    tb = np.where(first[None, :], 0.0, sin)
    tile = lambda t: jnp.asarray(np.tile(t, (1, LANES // WA_DH)), F32)
    return tile(cos), tile(ta), tile(tb)


def _rope_kernel(q_ref, k_ref, c_ref, a_ref, b_ref, qo_ref, ko_ref):
    c, a, b = c_ref[...], a_ref[...], b_ref[...]
    nf = WA_DH // 4

    def rot(u):
        return u * c + pltpu.roll(u, LANES - nf, 1) * a + pltpu.roll(u, nf, 1) * b

    for j in range(q_ref.shape[2] // LANES):
        qo_ref[0, :, j * LANES:(j + 1) * LANES] = rot(q_ref[0, :, j * LANES:(j + 1) * LANES])
    ko_ref[0] = rot(k_ref[0])


def _rope(p, tabs):
    b, l, _ = p.shape
    t = min(l, 512)
    qw = WA_HEADS * WA_DH
    tspec = pl.BlockSpec((t, LANES), lambda bi, i: (i, 0))
    return pl.pallas_call(
        _rope_kernel,
        grid=(b, l // t),
        in_specs=[pl.BlockSpec((1, t, qw), lambda bi, i: (bi, i, OFF["c_q"] // qw)),
                  pl.BlockSpec((1, t, LANES), lambda bi, i: (bi, i, OFF["c_k"] // LANES)),
                  tspec, tspec, tspec],
        out_specs=[pl.BlockSpec((1, t, qw), lambda bi, i: (bi, i, 0)),
                   pl.BlockSpec((1, t, LANES), lambda bi, i: (bi, i, 0))],
        out_shape=[jax.ShapeDtypeStruct((b, l, qw), F32), jax.ShapeDtypeStruct((b, l, LANES), F32)],
        compiler_params=_cparams(("parallel", "parallel")),
        name="rope",
    )(p, p, *tabs)


def _wa_kernel(q_ref, k_ref, v_ref, kc_ref, vc_ref, sink_ref, o_ref, *, l):
    n = pl.program_id(1)
    blk = WA_BLOCK
    span = 3 * blk
    start = jnp.clip((n - 1) * blk, 0, l - span)
    ks = pl.ds(pl.multiple_of(start, blk), span)
    kw = k_ref[0, ks, :].astype(BF16)
    vw = v_ref[0, ks, :].astype(BF16)
    kc = kc_ref[0].astype(BF16)
    vc = vc_ref[0].astype(BF16)
    ngrp = WA_HEADS // 2
    lane = lax.broadcasted_iota(jnp.int32, (blk, LANES), 1)
    ii = lax.broadcasted_iota(jnp.int32, (ngrp * blk, span), 0) & (blk - 1)
    jj = lax.broadcasted_iota(jnp.int32, (ngrp * blk, span), 1)
    dist = (jj + start) - (ii + n * blk)
    valid = jnp.abs(dist) <= WA_RADIUS
    halves = []
    for half in range(2):
        keep = (lane < 64) if half == 0 else (lane >= 64)
        qs = jnp.concatenate(
            [jnp.where(keep, q_ref[0, :, j * LANES:(j + 1) * LANES] * (WA_DH ** -0.5), 0.0) for j in range(ngrp)],
            0).astype(BF16)
        s_loc = jnp.where(valid, _dot_nt(qs, kw), -jnp.inf)
        s_ctx = _dot_nt(qs, kc)
        halves.append(_softmax_av([(s_loc, vw), (s_ctx, vc)], sink_ref[half][:, 0:1]))
    for j in range(ngrp):
        o_ref[0, :, j * LANES:(j + 1) * LANES] = jnp.where(lane < 64, halves[0][j * blk:(j + 1) * blk],
                                                           halves[1][j * blk:(j + 1) * blk])


def _wa_attention(qr, kr, p, pc, sink_rows):
    b, l, qw = qr.shape
    lc = pc.shape[1]
    return pl.pallas_call(
        functools.partial(_wa_kernel, l=l),
        grid=(b, l // WA_BLOCK),
        in_specs=[pl.BlockSpec((1, WA_BLOCK, qw), lambda bi, n: (bi, n, 0)),
                  pl.BlockSpec((1, l, LANES), lambda bi, n: (bi, 0, 0)),
                  pl.BlockSpec((1, l, LANES), lambda bi, n: (bi, 0, OFF["c_v"] // LANES)),
                  pl.BlockSpec((1, lc, LANES), lambda bi, n: (bi, 0, OFF["c_k"] // LANES)),
                  pl.BlockSpec((1, lc, LANES), lambda bi, n: (bi, 0, OFF["c_v"] // LANES)),
                  pl.BlockSpec(sink_rows.shape, lambda bi, n: (0, 0, 0))],
        out_specs=pl.BlockSpec((1, WA_BLOCK, qw), lambda bi, n: (bi, n, 0)),
        out_shape=jax.ShapeDtypeStruct((b, l, qw), F32),
        compiler_params=_cparams(("parallel", "arbitrary")),
        name="wa_attention",
    )(qr, kr, p, pc, pc, sink_rows)


def _rms_gate(o, og):
    outs = []
    for hd in range(o.shape[1] // LANES):
        oh = o[:, hd * LANES:(hd + 1) * LANES]
        r = lax.rsqrt(jnp.mean(oh * oh, axis=-1, keepdims=True) + 1e-6)
        outs.append(oh * r * _silu(og[:, hd * LANES:(hd + 1) * LANES]))
    return jnp.concatenate(outs, 1)


def _merge_kernel(oaf_ref, oab_ref, ag_ref, ob_ref, oc_ref, odf_ref, odb_ref, dg_ref, gate_ref, x_ref, mod_ref,
                  wbr_ref, wout_ref, lng_ref, lnb_ref, wr_ref, br_ref, o_ref, g_ref):
    branches = (_rms_gate(oaf_ref[0] + oab_ref[0], ag_ref[0]), ob_ref[0], oc_ref[0],
                _rms_gate(odf_ref[0] + odb_ref[0], dg_ref[0]))
    acc = 0.0
    for i, br in enumerate(branches):
        y = _dot(br.astype(BF16), wbr_ref[i])
        acc = acc + _sigmoid(gate_ref[0, :, i * D_MODEL:(i + 1) * D_MODEL]) * y
    mix = _dot(acc.astype(BF16), wout_ref[...])
    m = mod_ref[0]
    z = ALPHA * x_ref[...] + m[2:3] * mix
    xn = _layer_norm(z, lng_ref[...], lnb_ref[...])
    o_ref[...] = xn
    h = xn * (1.0 + m[4:5]) + m[3:4]
    lane = lax.broadcasted_iota(jnp.int32, (h.shape[0], LANES), 1)
    gl = jnp.where(lane < N_GROUPS, _dot3(h, wr_ref[...]) + br_ref[...], -jnp.inf)
    gmax = jnp.max(gl, axis=-1, keepdims=True)
    g_ref[...] = jnp.min(jnp.where(gl == gmax, lane, LANES), axis=-1, keepdims=True)


def _merge_out_ln(o_a, o_b, o_c, o_d, p, x, mod, wbr, wout, lng, lnb, wr, br):
    b, l, _ = p.shape
    d = x.shape[1]
    tm = min(l, 256)
    nt = l // tm
    gw = N_BRANCH * D_MODEL
    bspec = pl.BlockSpec((1, tm, BRANCH_W), lambda bi, i: (bi, i, 0))

    def pcol(name):
        blk = OFF[name] // BRANCH_W
        return pl.BlockSpec((1, tm, BRANCH_W), lambda bi, i: (bi, i, blk))

    return pl.pallas_call(
        _merge_kernel,
        grid=(b, l // tm),
        in_specs=[bspec, bspec, pcol("a_g"), bspec, bspec, bspec, bspec, pcol("d_g"),
                  pl.BlockSpec((1, tm, gw), lambda bi, i: (bi, i, OFF["gate"] // gw)),
                  pl.BlockSpec((tm, d), lambda bi, i: (bi * nt + i, 0)),
                  pl.BlockSpec((1, 8, d), lambda bi, i: (bi, 0, 0)),
                  pl.BlockSpec((N_BRANCH, BRANCH_W, d), lambda bi, i: (0, 0, 0)),
                  pl.BlockSpec((d, d), lambda bi, i: (0, 0)),
                  pl.BlockSpec((1, d), lambda bi, i: (0, 0)),
                  pl.BlockSpec((1, d), lambda bi, i: (0, 0)),
                  pl.BlockSpec((d, LANES), lambda bi, i: (0, 0)),
                  pl.BlockSpec((1, LANES), lambda bi, i: (0, 0))],
        out_specs=[pl.BlockSpec((tm, d), lambda bi, i: (bi * nt + i, 0)),
                   pl.BlockSpec((tm, 1), lambda bi, i: (bi * nt + i, 0))],
        out_shape=[jax.ShapeDtypeStruct((b * l, d), F32), jax.ShapeDtypeStruct((b * l, 1), jnp.int32)],
        compiler_params=_cparams(("parallel", "parallel")),
        name="merge_out_ln",
    )(o_a[0], o_a[1], p, o_b, o_c, o_d[0], o_d[1], p, p, x, mod, wbr, wout, lng, lnb, wr, br)


def _moe_plan(gidx, n):
    tmax = n // MOE_TM + N_GROUPS
    i32 = jnp.int32
    key = gidx.reshape(n)
    tok = jnp.arange(n, dtype=i32)
    onehot = (key[:, None] == jnp.arange(N_GROUPS, dtype=i32)[None, :]).astype(i32)
    cum = jnp.cumsum(onehot, axis=0)
    counts = cum[-1]
    rank = jnp.sum(onehot * cum, axis=1) - 1
    tiles = (counts + MOE_TM - 1) // MOE_TM
    tile_end = jnp.cumsum(tiles)
    ntiles = tile_end[-1]
    row_start = (tile_end - tiles) * MOE_TM
    ppos = jnp.sum(onehot * row_start[None, :], axis=1) + rank
    placed = jnp.zeros((tmax * MOE_TM,), i32).at[ppos].set(tok).reshape(tmax, MOE_TM)
    last_tok = jnp.max(onehot * tok[:, None], axis=0)
    t = jnp.arange(tmax, dtype=i32)
    tc = jnp.minimum(t, ntiles - 1)
    seg = jnp.minimum(jnp.sum((tile_end[None, :] <= tc[:, None]).astype(i32), axis=1), N_GROUPS - 1)
    in_seg = tc - (tile_end[seg] - tiles[seg])
    off = in_seg[:, None] * MOE_TM + jnp.arange(MOE_TM, dtype=i32)[None, :]
    valid = jnp.logical_and(off < counts[seg][:, None], (t < ntiles)[:, None])
    rows_in = jnp.where(valid, placed, last_tok[seg][:, None])
    rows_out = jnp.where(valid, placed, n + jnp.arange(MOE_TM, dtype=i32)[None, :])
    return seg, rows_in, rows_out


def _moe_tile_kernel(tg_ref, rin_ref, rnext_ref, rout_ref, rvec_ref, x_hbm, mod_ref, wr_ref, br_ref, wg_ref,
                     wu_ref, wd_ref, lng_ref, lnb_ref, o_hbm, xbuf, ybuf, sem_in, sem_out, *, l):
    t = pl.program_id(0)
    last = pl.num_programs(0) - 1
    tm = MOE_TM
    slot = lax.rem(t, 2)

    def row_in(rows_ref, r, s):
        return pltpu.make_async_copy(x_hbm.at[pl.ds(rows_ref[0, 0, r], 1)], xbuf.at[s, pl.ds(r, 1)], sem_in.at[s])

    def row_out(r):
        return pltpu.make_async_copy(ybuf.at[pl.ds(r, 1)], o_hbm.at[pl.ds(rout_ref[0, 0, r], 1)], sem_out)

    @pl.when(t == 0)
    def _():
        for r in range(tm):
            row_in(rin_ref, r, 0).start()
        ybuf[...] = jnp.zeros_like(ybuf)
        spare = pltpu.make_async_copy(ybuf, o_hbm.at[pl.ds(o_hbm.shape[0] - tm, tm)], sem_out)
        spare.start()
        spare.wait()

    for r in range(tm):
        row_in(rin_ref, r, slot).wait()
    for r in range(tm):
        row_in(rnext_ref, r, 1 - slot).start()

    g = tg_ref[t]
    x = xbuf[slot]
    tok = rvec_ref[0]
    bidx = jnp.zeros((tm, 1), jnp.int32)
    nb = mod_ref.shape[0]
    for k in range(1, nb):
        bidx = bidx + (tok >= k * l).astype(jnp.int32)

    def mod_row(j):
        out = mod_ref[0, j:j + 1, :]
        for k in range(1, nb):
            out = jnp.where(bidx == k, mod_ref[k, j:j + 1, :], out)
        return out

    h = x * (1.0 + mod_row(4)) + mod_row(3)
    lane = lax.broadcasted_iota(jnp.int32, (tm, LANES), 1)
    ninf = -jnp.inf
    lg = _dot3(h, wr_ref[...]) + br_ref[...]
    gl = jnp.where(lane < N_GROUPS, lg, ninf)
    gmax = jnp.max(gl, axis=-1, keepdims=True)
    gsel = jnp.sum(jnp.where(lane == g, lg, 0.0), axis=-1, keepdims=True)
    gw = jnp.exp(gsel - gmax) / jnp.sum(jnp.exp(gl - gmax), axis=-1, keepdims=True)
    e0 = N_GROUPS + g * EXPERTS_PER_GROUP
    ingrp = jnp.logical_and(lane >= e0, lane < e0 + EXPERTS_PER_GROUP)
    e1 = jnp.where(ingrp, lg, ninf)
    m1 = jnp.max(e1, axis=-1, keepdims=True)
    i1 = jnp.min(jnp.where(e1 == m1, lane, LANES), axis=-1, keepdims=True)
    e2 = jnp.where(lane == i1, ninf, e1)
    m2 = jnp.max(e2, axis=-1, keepdims=True)
    i2 = jnp.min(jnp.where(e2 == m2, lane, LANES), axis=-1, keepdims=True)
    tt = jnp.exp(m2 - m1)
    wts = gw * (jnp.where(lane == i1, 1.0 / (1.0 + tt), 0.0) + jnp.where(lane == i2, tt / (1.0 + tt), 0.0))

    hb = h.astype(BF16)
    acc = jnp.zeros((tm, D_MODEL), F32)
    for j in range(EXPERTS_PER_GROUP):
        wcol = jnp.sum(jnp.where(lane == e0 + j, wts, 0.0), axis=-1, keepdims=True)
        hid = _silu(_dot(hb, wg_ref[j])) * _dot(hb, wu_ref[j]) * wcol
        acc = acc + _dot(hid.astype(BF16), wd_ref[j])
    y = _layer_norm(ALPHA * x + mod_row(5) * acc, lng_ref[...], lnb_ref[...])

    @pl.when(t > 0)
    def _():
        for r in range(tm):
            row_out(r).wait()

    ybuf[...] = y
    for r in range(tm):
        row_out(r).start()

    @pl.when(t == last)
    def _():
        for r in range(tm):
            row_out(r).wait()
        for r in range(tm):
            row_in(rnext_ref, r, 1 - slot).wait()


def _moe_ln(x, gidx, b, l, mod, wr, br, wg, wu, wd, lng, lnb):
    d = x.shape[1]
    n = b * l
    tile_group, rows_in, rows_out = _moe_plan(gidx, n)
    tmax = rows_in.shape[0]
    cst = lambda shape: pl.BlockSpec(shape, lambda t, tg: tuple(0 for _ in shape))
    smem_rows = lambda fn: pl.BlockSpec((1, 1, MOE_TM), fn, memory_space=pltpu.SMEM)
    grid_spec = pltpu.PrefetchScalarGridSpec(
        num_scalar_prefetch=1,
        grid=(tmax,),
        in_specs=[smem_rows(lambda t, tg: (t, 0, 0)),
                  smem_rows(lambda t, tg: (jnp.minimum(t + 1, tmax - 1), 0, 0)),
                  smem_rows(lambda t, tg: (t, 0, 0)),
                  pl.BlockSpec((1, MOE_TM, 1), lambda t, tg: (t, 0, 0)),
                  pl.BlockSpec(memory_space=pl.ANY),
                  cst(mod.shape), cst((d, LANES)), cst((1, LANES)),
                  pl.BlockSpec((EXPERTS_PER_GROUP, d, EXPERT_HIDDEN), lambda t, tg: (tg[t], 0, 0)),
                  pl.BlockSpec((EXPERTS_PER_GROUP, d, EXPERT_HIDDEN), lambda t, tg: (tg[t], 0, 0)),
                  pl.BlockSpec((EXPERTS_PER_GROUP, EXPERT_HIDDEN, d), lambda t, tg: (tg[t], 0, 0)),
                  cst((1, d)), cst((1, d))],
        out_specs=pl.BlockSpec(memory_space=pl.ANY),
        scratch_shapes=[pltpu.VMEM((2, MOE_TM, d), F32), pltpu.VMEM((MOE_TM, d), F32),
                        pltpu.SemaphoreType.DMA((2,)), pltpu.SemaphoreType.DMA],
    )
    rin3 = rows_in.reshape(tmax, 1, MOE_TM)
    return pl.pallas_call(
        functools.partial(_moe_tile_kernel, l=l),
        grid_spec=grid_spec,
        out_shape=jax.ShapeDtypeStruct((n + MOE_TM, d), F32),
        compiler_params=_cparams(("arbitrary",)),
        name="moe_ln",
    )(tile_group, rin3, rin3, rows_out.reshape(tmax, 1, MOE_TM), rows_in.reshape(tmax, MOE_TM, 1), x, mod, wr, br,
      wg, wu, wd, lng, lnb)


def _pad_cols(w, n):
    return jnp.pad(w, ((0, 0), (0, n - w.shape[1])))


def _wa_head_pairs(w, axis):
    shp = w.shape
    w = w.reshape(shp[:axis] + (2, WA_HEADS // 2, WA_DH) + shp[axis + 1:])
    w = jnp.swapaxes(w, axis, axis + 1)
    return w.reshape(shp)


def _prep_w_in(w):
    segs = []
    for name in _NEW_ORDER:
        s = w[:, _ORIG_OFF[name]:_ORIG_OFF[name] + _SIZES[name]]
        if name == "c_q":
            s = _wa_head_pairs(s, 1)
        segs.append(s)
    segs.append(jnp.zeros((w.shape[0], NP - D_IN), w.dtype))
    return jnp.concatenate(segs, 1).astype(BF16)


def _prep_gla_gate(lr_w, lr_b):
    w = jnp.zeros((2, 2, LANES, LANES), F32)
    for z in range(2):
        for n in range(2):
            w = w.at[z, n, z * GLA_RANK:(z + 1) * GLA_RANK, :].set(lr_w[z][:, n * LANES:(n + 1) * LANES])
    return w.astype(BF16), lr_b.reshape(2, 2, 1, LANES)


def _sink_rows(sink, rows_per_head):
    s = sink.reshape(2, 4)
    return jnp.broadcast_to(s[:, :, None, None], (2, 4, rows_per_head, LANES)).reshape(2, 4 * rows_per_head, LANES)


def kernel(x, c, ctx, c_ctx, w_mod, b_mod, w_in, gla_lr_w, gla_lr_b, hg_gamma, na_rpb, wa_sink, w_branch, w_out,
           ln_g, ln_b, moe_w_group, moe_b_group, moe_w_expert, moe_b_expert, moe_w_gate, moe_w_up, moe_w_down):
    b, l, d = x.shape
    lc = ctx.shape[1]
    assert d == D_MODEL and b + 1 <= 8 and l % 512 == 0 and l >= 3 * WA_BLOCK and lc % SCAN_C == 0

    cc = jnp.concatenate([c, c_ctx[None], jnp.zeros((8 - b - 1, d), F32)], 0)
    mods = _modulation(cc, w_mod, b_mod)
    cum = jnp.cumsum(jax.nn.softmax(hg_gamma.astype(F32), axis=1), axis=1)
    lower_bounds = cum - cum[:, :1]
    rope_tabs = _rope_tables(l)

    x = x.reshape(b * l, d)
    cx = ctx.reshape(b * lc, d)
    for li in range(DEPTH):
        with_ctx = li < DEPTH - 1
        mod6 = mods[li].reshape(8, 6, d)
        mod = jnp.pad(mod6[:b], ((0, 0), (0, 2), (0, 0)))
        modc = jnp.broadcast_to(jnp.pad(mod6[b], ((0, 2), (0, 0)))[None], (b, 8, d))
        w_in_p = _prep_w_in(w_in[li])
        p = _in_proj(x, b, l, mod, w_in_p)
        pc = _in_proj(cx, b, lc, modc, w_in_p)

        gla_aux = _prep_gla_gate(gla_lr_w[li], gla_lr_b[li])
        o_a, oc_a = _recurrent_branch("gla", p, pc, gla_aux)
        hg_aux = (lower_bounds[:, li].reshape(2, 1, HG_HEADS * HG_DK),)
        o_d, oc_d = _recurrent_branch("hgrn", p, pc, hg_aux)

        o_b = _na_attention(p, pc, _na_bias_table(na_rpb[li]))
        qr, kr = _rope(p, rope_tabs)
        o_c = _wa_attention(qr, kr, p, pc, _sink_rows(wa_sink[li], WA_BLOCK))

        wbr = jnp.stack([w_branch[li, 0], w_branch[li, 1], _wa_head_pairs(w_branch[li, 2], 0),
                         w_branch[li, 3]]).astype(BF16)
        wout = w_out[li].astype(BF16)
        lng0, lnb0 = ln_g[li, 0][None], ln_b[li, 0][None]
        wr = _pad_cols(jnp.concatenate([moe_w_group[li], moe_w_expert[li]], 1), LANES)
        br = _pad_cols(jnp.concatenate([moe_b_group[li], moe_b_expert[li]])[None], LANES)
        x, gidx = _merge_out_ln(o_a, o_b, o_c, o_d, p, x, mod, wbr, wout, lng0, lnb0, wr, br)

        wg = moe_w_gate[li].reshape(N_EXPERTS, d, EXPERT_HIDDEN).astype(BF16)
        wu = moe_w_up[li].reshape(N_EXPERTS, d, EXPERT_HIDDEN).astype(BF16)
        wd = moe_w_down[li].reshape(N_EXPERTS, EXPERT_HIDDEN, d).astype(BF16)
        lng1, lnb1 = ln_g[li, 1][None], ln_b[li, 1][None]
        moe = functools.partial(_moe_ln, wr=wr, br=br, wg=wg, wu=wu, wd=wd, lng=lng1, lnb=lnb1)
        if with_ctx:
            oc_b = _ctx_attention(pc, "b_q", "b_k", "b_v", True, None)
            oc_c = _ctx_attention(pc, "c_q", "c_k", "c_v", False,
                                  jnp.broadcast_to(wa_sink[li].reshape(2, 4).T[:, :, None, None], (4, 2, lc, LANES)))
            cx, gidx_c = _merge_out_ln(oc_a, oc_b, oc_c, oc_d, pc, cx, modc, wbr, wout, lng0, lnb0, wr, br)
            cx = moe(cx, gidx_c, b, lc, modc)
        x = moe(x, gidx, b, l, mod)
    return x[:b * l].reshape(b, l, d)
```

```python
import functools

import numpy as np
import jax
import jax.numpy as jnp
from jax import lax
from jax.experimental import pallas as pl
from jax.experimental.pallas import tpu as pltpu

F32 = jnp.float32
BF16 = jnp.bfloat16
HIGHEST = lax.Precision.HIGHEST

LANES = 128
VMEM_LIMIT = 56 * 1024 * 1024

D_MODEL = 1024
DEPTH = 2
GRID_W = 64
GLA_HEADS, GLA_DK, GLA_DV, GLA_RANK, GLA_TAU = 4, 64, 128, 16, 16.0
NA_HEADS, NA_DH, NA_WIN_R, NA_WIN_C = 8, 64, 8, 16
WA_HEADS, WA_KV_HEADS, WA_DH, WA_RADIUS, WA_BLOCK = 8, 2, 64, 128, 128
ROPE_BASE = 10000.0
HG_HEADS, HG_DK, HG_DV = 4, 128, 128
N_BRANCH, BRANCH_W = 4, 512
N_GROUPS, EXPERTS_PER_GROUP, EXPERT_HIDDEN = 4, 8, 256
N_EXPERTS = N_GROUPS * EXPERTS_PER_GROUP
LN_EPS = 1e-5
ALPHA = (2.0 * DEPTH) ** 0.25

_IN_COLS = (
    ("a_q", 256), ("a_k", 256), ("a_v", 512), ("a_g", 512), ("a_lr", 32),
    ("b_q", 512), ("b_k", 512), ("b_v", 512),
    ("c_q", 512), ("c_k", 128), ("c_v", 128),
    ("d_q", 512), ("d_f", 1024), ("d_i", 512), ("d_g", 512), ("gate", 4096),
)
_ORIG_OFF = {}
_o = 0
for _n, _s in _IN_COLS:
    _ORIG_OFF[_n] = _o
    _o += _s
D_IN = _o
_NEW_ORDER = ("gate", "a_q", "a_k", "a_v", "a_g", "b_q", "b_k", "b_v", "c_q",
              "d_q", "d_f", "d_i", "d_g", "c_k", "c_v", "a_lr")
_SIZES = dict(_IN_COLS)
OFF = {}
_o = 0
for _n in _NEW_ORDER:
    OFF[_n] = _o
    _o += _SIZES[_n]
PROJ_TN = 768
NP = ((_o + PROJ_TN - 1) // PROJ_TN) * PROJ_TN

SCAN_C = 128
SCAN_LEVELS = (64, 32, 16, 8, 4, 2, 1)

MOE_TM = 256

NA_RB = 4
NA_KR = NA_RB + NA_WIN_R - 1


def _cparams(sem, vmem=VMEM_LIMIT):
    return pltpu.CompilerParams(dimension_semantics=sem, vmem_limit_bytes=vmem)


def _dot(a, b, **kw):
    return jnp.dot(a, b, preferred_element_type=F32, **kw)


def _dot_nt(a, b):
    return lax.dot_general(a, b, (((1,), (1,)), ((), ())), preferred_element_type=F32)


def _dot_tn(a, b):
    return lax.dot_general(a, b, (((0,), (0,)), ((), ())), preferred_element_type=F32)


def _dot3(a, w):
    a1 = a.astype(BF16)
    a2 = (a - a1.astype(F32)).astype(BF16)
    w1 = w.astype(BF16)
    w2 = (w - w1.astype(F32)).astype(BF16)
    return _dot(a1, w1) + (_dot(a1, w2) + _dot(a2, w1))


def _sigmoid(x):
    return 1.0 / (1.0 + jnp.exp(-x))


def _silu(x):
    return x * _sigmoid(x)


def _log_sigmoid(x):
    return jnp.minimum(x, 0.0) - jnp.log(1.0 + jnp.exp(-jnp.abs(x)))


def _layer_norm(z, g, b):
    mu = jnp.mean(z, axis=-1, keepdims=True)
    zc = z - mu
    var = jnp.mean(zc * zc, axis=-1, keepdims=True)
    return zc * lax.rsqrt(var + LN_EPS) * g + b


def _mod_kernel(c_ref, w_ref, b_ref, o_ref):
    c = c_ref[...]
    o_ref[0] = _dot(_silu(c), w_ref[0], precision=HIGHEST) + b_ref[0]


def _modulation(cc, w_mod, b_mod):
    depth, d, n = w_mod.shape
    tn = 1536
    return pl.pallas_call(
        _mod_kernel,
        grid=(depth, n // tn),
        in_specs=[pl.BlockSpec((8, d), lambda l, j: (0, 0)),
                  pl.BlockSpec((1, d, tn), lambda l, j: (l, 0, j)),
                  pl.BlockSpec((1, 1, tn), lambda l, j: (l, 0, j))],
        out_specs=pl.BlockSpec((1, 8, tn), lambda l, j: (l, 0, j)),
        out_shape=jax.ShapeDtypeStruct((depth, 8, n), F32),
        compiler_params=_cparams(("parallel", "parallel")),
        name="modulation",
    )(cc, w_mod, b_mod.reshape(depth, 1, n))


def _in_proj_kernel(x_ref, mod_ref, w_ref, o_ref, h_ref):
    @pl.when(pl.program_id(2) == 0)
    def _():
        m = mod_ref[0]
        h_ref[...] = (x_ref[...] * (1.0 + m[1:2]) + m[0:1]).astype(BF16)

    o_ref[0] = _dot(h_ref[...], w_ref[...])


def _in_proj(x, b, l, mod, w):
    d = x.shape[1]
    tm = min(l, 2048)
    nt = l // tm
    return pl.pallas_call(
        _in_proj_kernel,
        grid=(b, nt, NP // PROJ_TN),
        in_specs=[pl.BlockSpec((tm, d), lambda bi, i, j: (bi * nt + i, 0)),
                  pl.BlockSpec((1, 8, d), lambda bi, i, j: (bi, 0, 0)),
                  pl.BlockSpec((d, PROJ_TN), lambda bi, i, j: (0, j))],
        out_specs=pl.BlockSpec((1, tm, PROJ_TN), lambda bi, i, j: (bi, i, j)),
        out_shape=jax.ShapeDtypeStruct((b, l, NP), F32),
        scratch_shapes=[pltpu.VMEM((tm, d), BF16)],
        compiler_params=_cparams(("parallel", "parallel", "arbitrary")),
        name="in_proj",
    )(x, mod, w)


def _scan_consts(nh):
    c = SCAN_C
    i = np.arange(c)[:, None]
    j = np.arange(c)[None, :]
    tris, sels, msks = [], [], []
    for reverse in (False, True):
        tris.append((j >= i) if reverse else (j <= i))
        sel, masks = [], []
        for h in SCAN_LEVELS:
            same = (i // (2 * h)) == (j // (2 * h))
            base = (np.arange(c) // (2 * h)) * (2 * h)
            if reverse:
                pos = base + h
                m = same & ((i % (2 * h)) < h) & ((j % (2 * h)) >= h)
            else:
                pos = base + h - 1
                m = same & ((i % (2 * h)) >= h) & ((j % (2 * h)) < h)
            sel.append(j == pos[:, None])
            masks.append(m)
        masks.append(i == j)
        sels.append(np.concatenate(sel, 0))
        msks.append(np.stack([np.tile(m, (nh, 1)) for m in masks]))
    return (jnp.asarray(np.stack(tris), BF16), jnp.asarray(np.stack(sels), BF16),
            jnp.asarray(np.stack(msks), F32))


def _split3(x):
    a = x.astype(BF16)
    r = x - a.astype(F32)
    b = r.astype(BF16)
    c = (r - b.astype(F32)).astype(BF16)
    return jnp.concatenate([a, b, c], 1)


def _scan_kernel(*refs, mode, t):
    nh = 2 if mode == "gla" else 1
    it = iter(refs)
    q_refs, k_refs, v_refs = (next(it), next(it)), (next(it), next(it)), (next(it), next(it))
    if mode == "gla":
        lr_refs = (next(it), next(it))
        wlr_ref, blr_ref = next(it), next(it)
    else:
        lb_ref = next(it)
    init_ref, tri_ref, sel_ref, msk_ref = next(it), next(it), next(it), next(it)
    o_refs = (next(it), next(it))
    fin_ref = next(it)
    st_refs = (next(it), next(it))

    @pl.when(pl.program_id(2) == 0)
    def _():
        for z in range(2):
            st_refs[z][...] = init_ref[0, 0, z]

    c = SCAN_C
    nc = t // c
    nlev = len(SCAN_LEVELS)
    row = lax.broadcasted_iota(jnp.int32, (c, LANES), 0)
    lane = lax.broadcasted_iota(jnp.int32, (c, LANES), 1)

    def stack(x):
        if nh == 1:
            return x.astype(BF16)
        return jnp.concatenate([jnp.where(lane < 64, x, 0.0), jnp.where(lane >= 64, x, 0.0)], 0).astype(BF16)

    order = [[ci if z == 0 else nc - 1 - ci for ci in range(nc)] for z in range(2)]
    items = []
    for z in range(2):
        for cc in order[z]:
            sl = pl.ds(cc * c, c)
            if mode == "gla":
                qq = q_refs[z][0, sl, :] * (GLA_DK ** -0.5)
                kk = k_refs[z][0, sl, :]
                logit = _dot(lr_refs[z][0, sl, :].astype(BF16), wlr_ref[z, 0]) + blr_ref[z, 0]
                g = _log_sigmoid(logit) * (1.0 / GLA_TAU)
            else:
                qq = _silu(q_refs[z][0, sl, :])
                lb = lb_ref[z]
                f = k_refs[z][0, sl, :]
                ef = jnp.exp(-jnp.abs(f))
                rf = 1.0 / (1.0 + ef)
                ls = jnp.log1p(-lb) + jnp.minimum(f, 0.0) - jnp.log(1.0 + ef)
                llb = jnp.log(lb)
                g = jnp.maximum(llb, ls) + jnp.log(1.0 + jnp.exp(-jnp.abs(llb - ls)))
                kk = (1.0 - lb) * jnp.where(f >= 0.0, ef * rf, rf)
            items.append(dict(z=z, sl=sl, qq=qq, kk=kk, g=g))

    for z in range(2):
        mine = [it_ for it_ in items if it_["z"] == z]
        b3 = _dot(tri_ref[z], jnp.concatenate([_split3(it_["g"]) for it_ in mine], 1))
        for n, it_ in enumerate(mine):
            o3 = 3 * LANES * n
            it_["bc"] = b3[:, o3:o3 + LANES] + b3[:, o3 + LANES:o3 + 2 * LANES] + b3[:, o3 + 2 * LANES:o3 + 3 * LANES]
        bh = _dot(sel_ref[z], jnp.concatenate([it_["bc"] for it_ in mine], 1).astype(BF16))
        for n, it_ in enumerate(mine):
            it_["bh"] = bh[:, n * LANES:(n + 1) * LANES]

    for it_ in items:
        it_["a"] = jnp.where(msk_ref[it_["z"], nlev] > 0.0, _dot_nt(stack(it_["qq"]), it_["kk"].astype(BF16)), 0.0)
    for li, h in enumerate(SCAN_LEVELS):
        upper = (row & (2 * h - 1)) >= h
        for it_ in items:
            z = it_["z"]
            qrow = jnp.logical_not(upper) if z == 1 else upper
            d = it_["bc"] - it_["bh"][li * c:(li + 1) * c]
            e = jnp.exp(jnp.where(qrow, d, -d))
            zz = jnp.where(qrow, it_["qq"], it_["kk"]) * e
            it_["a"] = jnp.where(msk_ref[z, li] > 0.0, _dot_nt(stack(zz), zz.astype(BF16)), it_["a"])

    for it_ in items:
        z = it_["z"]
        ab = it_["a"].astype(BF16)
        vb = v_refs[z][0, it_["sl"], :].astype(BF16)
        it_["o"] = jnp.concatenate(
            [_dot(ab[hd * c:(hd + 1) * c], vb[:, hd * LANES:(hd + 1) * LANES]) for hd in range(nh)], 1)
        bc = it_["bc"]
        blast = bc[0:1] if z == 1 else bc[c - 1:c]
        it_["qin"] = (it_["qq"] * jnp.exp(bc)).astype(BF16)
        upd = _dot_tn(vb, (it_["kk"] * jnp.exp(blast - bc)).astype(BF16))
        if nh == 2:
            r2 = lax.broadcasted_iota(jnp.int32, (2 * LANES, LANES), 0)
            l2 = lax.broadcasted_iota(jnp.int32, (2 * LANES, LANES), 1)
            upd = jnp.where((r2 < LANES) == (l2 < 64), upd, 0.0)
        it_["upd"] = upd
        it_["dec"] = jnp.exp(blast)

    for z in range(2):
        st = st_refs[z][...]
        for it_ in items:
            if it_["z"] != z:
                continue
            o_refs[z][0, it_["sl"], :] = it_["o"] + _dot_nt(it_["qin"], st.astype(BF16))
            st = st * it_["dec"] + it_["upd"]
        st_refs[z][...] = st

    @pl.when(pl.program_id(2) == pl.num_programs(2) - 1)
    def _():
        for z in range(2):
            fin_ref[0, 0, z] = st_refs[z][...]


def _scan_call(mode, p, init, aux):
    b, l, _ = p.shape
    nh = 2 if mode == "gla" else 1
    ninst = 2 if mode == "gla" else HG_HEADS
    t = min(l, 1024)
    nb = l // t
    w = LANES * nh

    def col(name, width, z, extra=0):
        base = (OFF[name] + extra) // width
        if z == 0:
            return pl.BlockSpec((1, t, width), lambda bi, n, i: (bi, i, base + n))
        return pl.BlockSpec((1, t, width), lambda bi, n, i: (bi, nb - 1 - i, base + n))

    def fixed(name, z):
        blk = OFF[name] // LANES
        if z == 0:
            return pl.BlockSpec((1, t, LANES), lambda bi, n, i: (bi, i, blk))
        return pl.BlockSpec((1, t, LANES), lambda bi, n, i: (bi, nb - 1 - i, blk))

    if mode == "gla":
        in_specs = [col("a_q", LANES, 0), col("a_q", LANES, 1), col("a_k", LANES, 0), col("a_k", LANES, 1),
                    col("a_v", w, 0), col("a_v", w, 1), fixed("a_lr", 0), fixed("a_lr", 1),
                    pl.BlockSpec((2, 1, LANES, LANES), lambda bi, n, i: (0, n, 0, 0)),
                    pl.BlockSpec((2, 1, 1, LANES), lambda bi, n, i: (0, n, 0, 0))]
        args = [p] * 8 + [aux[0], aux[1]]
    else:
        fw = HG_HEADS * HG_DK
        in_specs = [col("d_q", LANES, 0), col("d_q", LANES, 1), col("d_f", LANES, 0), col("d_f", LANES, 1, fw),
                    col("d_i", LANES, 0), col("d_i", LANES, 1),
                    pl.BlockSpec((2, 1, LANES), lambda bi, n, i: (0, 0, n))]
        args = [p] * 6 + [aux[0]]
    tri, sel, msk = _scan_consts(nh)
    in_specs += [pl.BlockSpec((1, 1, 2, w, LANES), lambda bi, n, i: (bi, n, 0, 0, 0)),
                 pl.BlockSpec(tri.shape, lambda bi, n, i: (0, 0, 0)),
                 pl.BlockSpec(sel.shape, lambda bi, n, i: (0, 0, 0)),
                 pl.BlockSpec(msk.shape, lambda bi, n, i: (0, 0, 0, 0))]
    args += [init, tri, sel, msk]
    return pl.pallas_call(
        functools.partial(_scan_kernel, mode=mode, t=t),
        grid=(b, ninst, nb),
        in_specs=in_specs,
        out_specs=[pl.BlockSpec((1, t, w), lambda bi, n, i: (bi, i, n)),
                   pl.BlockSpec((1, t, w), lambda bi, n, i: (bi, nb - 1 - i, n)),
                   pl.BlockSpec((1, 1, 2, w, LANES), lambda bi, n, i: (bi, n, 0, 0, 0))],
        out_shape=[jax.ShapeDtypeStruct((b, l, ninst * w), F32),
                   jax.ShapeDtypeStruct((b, l, ninst * w), F32),
                   jax.ShapeDtypeStruct((b, ninst, 2, w, LANES), F32)],
        scratch_shapes=[pltpu.VMEM((w, LANES), F32), pltpu.VMEM((w, LANES), F32)],
        compiler_params=_cparams(("parallel", "parallel", "arbitrary")),
        name=f"scan_{mode}",
    )(*args)


def _recurrent_branch(mode, p, pc, aux):
    b = p.shape[0]
    nh = 2 if mode == "gla" else 1
    ninst = 2 if mode == "gla" else HG_HEADS
    zero = jnp.zeros((b, ninst, 2, LANES * nh, LANES), F32)
    oc_f, oc_b, sc = _scan_call(mode, pc, zero, aux)
    o_f, o_b, _ = _scan_call(mode, p, sc, aux)
    return (o_f, o_b), (oc_f, oc_b)


def _na_bias_table(rpb):
    cols = np.arange(GRID_W)
    cs = np.clip(cols - NA_WIN_C // 2, 0, GRID_W - NA_WIN_C)
    kc = np.arange(GRID_W)
    valid = (kc[None, :] >= cs[:, None]) & (kc[None, :] < cs[:, None] + NA_WIN_C)
    col_idx = np.clip(kc[None, :] - cols[:, None] + NA_WIN_C - 1, 0, 2 * NA_WIN_C - 2)
    tiles = jnp.where(valid[None, None], rpb[:, :, col_idx], -jnp.inf)
    nrow = 2 * NA_WIN_R - 1
    tiles = jnp.concatenate([tiles, jnp.full((NA_HEADS, 1, GRID_W, GRID_W), -jnp.inf, F32)], 1)
    rl = np.arange(NA_RB)[:, None]
    kl = np.arange(NA_KR)[None, :]
    spec = ((np.zeros_like(rl), kl - rl + NA_WIN_R - 1),
            (rl, kl - rl + NA_WIN_R // 2 - 1),
            (np.full_like(rl, NA_RB - 1), kl - rl))
    tidx = np.stack([np.where((kl >= lo) & (kl < lo + NA_WIN_R), off, nrow) for lo, off in spec])
    assert tidx.min() >= 0 and tidx.max() <= nrow
    rows = [jnp.concatenate([tiles[:, int(tidx[cls, r, k])] for k in range(NA_KR)], axis=-1)
            for cls in range(3) for r in range(NA_RB)]
    return jnp.stack(rows, 1).reshape(NA_HEADS, 3, NA_RB * GRID_W, NA_KR * GRID_W).astype(F32)


def _softmax_av(parts, extra=None):
    m = functools.reduce(jnp.maximum, [jnp.max(s, axis=-1, keepdims=True) for s, _ in parts])
    if extra is not None:
        m = jnp.maximum(m, extra)
    den = 0.0
    acc = 0.0
    for s, v in parts:
        e = jnp.exp(s - m)
        den = den + jnp.sum(e, axis=-1, keepdims=True)
        acc = acc + _dot(e.astype(BF16), v)
    if extra is not None:
        den = den + jnp.exp(extra - m)
    return acc / den


def _na_kernel(q_ref, k_ref, v_ref, kc_ref, vc_ref, bias_ref, o_ref, *, rows):
    kc = kc_ref[0].astype(BF16)
    vc = vc_ref[0].astype(BF16)
    qn, kn = NA_RB * GRID_W, NA_KR * GRID_W
    nblk = rows // NA_RB
    lane = lax.broadcasted_iota(jnp.int32, (qn, LANES), 1)

    def body(bi, carry):
        r0 = bi * NA_RB
        w0 = jnp.clip(r0 - NA_WIN_R // 2, 0, rows - NA_KR)
        cls = jnp.where(bi == 0, 0, jnp.where(bi == nblk - 1, 2, 1))
        qs = pl.ds(pl.multiple_of(r0 * GRID_W, qn), qn)
        ks = pl.ds(pl.multiple_of(w0 * GRID_W, GRID_W), kn)
        q = q_ref[0, qs, :] * (NA_DH ** -0.5)
        kw = k_ref[0, ks, :].astype(BF16)
        vw = v_ref[0, ks, :].astype(BF16)
        outs = []
        for hd in range(2):
            qm = jnp.where((lane < 64) if hd == 0 else (lane >= 64), q, 0.0).astype(BF16)
            s_loc = _dot_nt(qm, kw) + bias_ref[hd, cls]
            s_ctx = _dot_nt(qm, kc)
            outs.append(_softmax_av([(s_loc, vw), (s_ctx, vc)]))
        o_ref[0, qs, :] = jnp.where(lane < 64, outs[0], outs[1])
        return carry

    lax.fori_loop(0, nblk, body, 0)


def _na_attention(p, pc, bias):
    b, l, _ = p.shape
    lc = pc.shape[1]
    rows = l // GRID_W
    assert rows % NA_RB == 0 and rows >= NA_KR + 1
    npair = NA_HEADS // 2

    def col(name):
        base = OFF[name] // LANES
        return pl.BlockSpec((1, l, LANES), lambda bi, n: (bi, 0, base + n))

    def ccol(name):
        base = OFF[name] // LANES
        return pl.BlockSpec((1, lc, LANES), lambda bi, n: (bi, 0, base + n))

    return pl.pallas_call(
        functools.partial(_na_kernel, rows=rows),
        grid=(b, npair),
        in_specs=[col("b_q"), col("b_k"), col("b_v"), ccol("b_k"), ccol("b_v"),
                  pl.BlockSpec((2, 3, NA_RB * GRID_W, NA_KR * GRID_W), lambda bi, n: (n, 0, 0, 0))],
        out_specs=pl.BlockSpec((1, l, LANES), lambda bi, n: (bi, 0, n)),
        out_shape=jax.ShapeDtypeStruct((b, l, NA_HEADS * NA_DH), F32),
        compiler_params=_cparams(("parallel", "parallel")),
        name="na_attention",
    )(p, p, p, pc, pc, bias)


def _ctx_attn_kernel(q_ref, k_ref, v_ref, sink_ref, o_ref, *, use_sink):
    lc = q_ref.shape[1]
    lane = lax.broadcasted_iota(jnp.int32, (lc, LANES), 1)
    q = q_ref[0] * (NA_DH ** -0.5)
    kb = k_ref[0].astype(BF16)
    vb = v_ref[0].astype(BF16)
    outs = []
    for hd in range(2):
        qm = jnp.where((lane < 64) if hd == 0 else (lane >= 64), q, 0.0).astype(BF16)
        s = _dot_nt(qm, kb)
        extra = sink_ref[0, hd][:, 0:1] if use_sink else None
        outs.append(_softmax_av([(s, vb)], extra))
    o_ref[0] = jnp.where(lane < 64, outs[0], outs[1])


def _ctx_attention(pc, qname, kname, vname, kv_per_q, sink_tab):
    b, lc, _ = pc.shape
    nblk = 4
    use_sink = sink_tab is not None
    if sink_tab is None:
        sink_tab = jnp.zeros((nblk, 2, lc, LANES), F32)
    qb, kb, vb = OFF[qname] // LANES, OFF[kname] // LANES, OFF[vname] // LANES
    kmul = 1 if kv_per_q else 0
    return pl.pallas_call(
        functools.partial(_ctx_attn_kernel, use_sink=use_sink),
        grid=(b, nblk),
        in_specs=[pl.BlockSpec((1, lc, LANES), lambda bi, n: (bi, 0, qb + n)),
                  pl.BlockSpec((1, lc, LANES), lambda bi, n: (bi, 0, kb + kmul * n)),
                  pl.BlockSpec((1, lc, LANES), lambda bi, n: (bi, 0, vb + kmul * n)),
                  pl.BlockSpec((1, 2, lc, LANES), lambda bi, n: (n, 0, 0, 0))],
        out_specs=pl.BlockSpec((1, lc, LANES), lambda bi, n: (bi, 0, n)),
        out_shape=jax.ShapeDtypeStruct((b, lc, nblk * LANES), F32),
        compiler_params=_cparams(("parallel", "parallel")),
        name="ctx_attention",
    )(pc, pc, pc, sink_tab)


def _rope_tables(l):
    nf = WA_DH // 4
    pos = np.arange(l)
    inv = ROPE_BASE ** (-np.arange(nf, dtype=np.float32) / nf)
    d = np.arange(WA_DH)
    p = np.where(d[None, :] < WA_DH // 2, (pos // GRID_W)[:, None], (pos % GRID_W)[:, None]).astype(np.float32)
    ang = p * inv[d % nf][None, :].astype(np.float32)
    cos, sin = np.cos(ang), np.sin(ang)
    first = (d % (2 * nf)) < nf
    ta = np.where(first[None, :], -sin, 0.0)
    tb = np.where(first[None, :], 0.0, sin)
    tile = lambda t: jnp.asarray(np.tile(t, (1, LANES // WA_DH)), F32)
    return tile(cos), tile(ta), tile(tb)


def _rope_kernel(q_ref, k_ref, c_ref, a_ref, b_ref, qo_ref, ko_ref):
    c, a, b = c_ref[...], a_ref[...], b_ref[...]
    nf = WA_DH // 4

    def rot(u):
        return u * c + pltpu.roll(u, LANES - nf, 1) * a + pltpu.roll(u, nf, 1) * b

    for j in range(q_ref.shape[2] // LANES):
        qo_ref[0, :, j * LANES:(j + 1) * LANES] = rot(q_ref[0, :, j * LANES:(j + 1) * LANES])
    ko_ref[0] = rot(k_ref[0])


def _rope(p, tabs):
    b, l, _ = p.shape
    t = min(l, 512)
    qw = WA_HEADS * WA_DH
    tspec = pl.BlockSpec((t, LANES), lambda bi, i: (i, 0))
    return pl.pallas_call(
        _rope_kernel,
        grid=(b, l // t),
        in_specs=[pl.BlockSpec((1, t, qw), lambda bi, i: (bi, i, OFF["c_q"] // qw)),
                  pl.BlockSpec((1, t, LANES), lambda bi, i: (bi, i, OFF["c_k"] // LANES)),
                  tspec, tspec, tspec],
        out_specs=[pl.BlockSpec((1, t, qw), lambda bi, i: (bi, i, 0)),
                   pl.BlockSpec((1, t, LANES), lambda bi, i: (bi, i, 0))],
        out_shape=[jax.ShapeDtypeStruct((b, l, qw), F32), jax.ShapeDtypeStruct((b, l, LANES), F32)],
        compiler_params=_cparams(("parallel", "parallel")),
        name="rope",
    )(p, p, *tabs)


def _wa_kernel(q_ref, k_ref, v_ref, kc_ref, vc_ref, sink_ref, o_ref, *, l):
    n = pl.program_id(1)
    blk = WA_BLOCK
    span = 3 * blk
    start = jnp.clip((n - 1) * blk, 0, l - span)
    ks = pl.ds(pl.multiple_of(start, blk), span)
    kw = k_ref[0, ks, :].astype(BF16)
    vw = v_ref[0, ks, :].astype(BF16)
    kc = kc_ref[0].astype(BF16)
    vc = vc_ref[0].astype(BF16)
    ngrp = WA_HEADS // 2
    lane = lax.broadcasted_iota(jnp.int32, (blk, LANES), 1)
    ii = lax.broadcasted_iota(jnp.int32, (ngrp * blk, span), 0) & (blk - 1)
    jj = lax.broadcasted_iota(jnp.int32, (ngrp * blk, span), 1)
    dist = (jj + start) - (ii + n * blk)
    valid = jnp.abs(dist) <= WA_RADIUS
    halves = []
    for half in range(2):
        keep = (lane < 64) if half == 0 else (lane >= 64)
        qs = jnp.concatenate(
            [jnp.where(keep, q_ref[0, :, j * LANES:(j + 1) * LANES] * (WA_DH ** -0.5), 0.0) for j in range(ngrp)],
            0).astype(BF16)
        s_loc = jnp.where(valid, _dot_nt(qs, kw), -jnp.inf)
        s_ctx = _dot_nt(qs, kc)
        halves.append(_softmax_av([(s_loc, vw), (s_ctx, vc)], sink_ref[half][:, 0:1]))
    for j in range(ngrp):
        o_ref[0, :, j * LANES:(j + 1) * LANES] = jnp.where(lane < 64, halves[0][j * blk:(j + 1) * blk],
                                                           halves[1][j * blk:(j + 1) * blk])


def _wa_attention(qr, kr, p, pc, sink_rows):
    b, l, qw = qr.shape
    lc = pc.shape[1]
    return pl.pallas_call(
        functools.partial(_wa_kernel, l=l),
        grid=(b, l // WA_BLOCK),
        in_specs=[pl.BlockSpec((1, WA_BLOCK, qw), lambda bi, n: (bi, n, 0)),
                  pl.BlockSpec((1, l, LANES), lambda bi, n: (bi, 0, 0)),
                  pl.BlockSpec((1, l, LANES), lambda bi, n: (bi, 0, OFF["c_v"] // LANES)),
                  pl.BlockSpec((1, lc, LANES), lambda bi, n: (bi, 0, OFF["c_k"] // LANES)),
                  pl.BlockSpec((1, lc, LANES), lambda bi, n: (bi, 0, OFF["c_v"] // LANES)),
                  pl.BlockSpec(sink_rows.shape, lambda bi, n: (0, 0, 0))],
        out_specs=pl.BlockSpec((1, WA_BLOCK, qw), lambda bi, n: (bi, n, 0)),
        out_shape=jax.ShapeDtypeStruct((b, l, qw), F32),
        compiler_params=_cparams(("parallel", "arbitrary")),
        name="wa_attention",
    )(qr, kr, p, pc, pc, sink_rows)


def _rms_gate(o, og):
    outs = []
    for hd in range(o.shape[1] // LANES):
        oh = o[:, hd * LANES:(hd + 1) * LANES]
        r = lax.rsqrt(jnp.mean(oh * oh, axis=-1, keepdims=True) + 1e-6)
        outs.append(oh * r * _silu(og[:, hd * LANES:(hd + 1) * LANES]))
    return jnp.concatenate(outs, 1)


def _merge_kernel(oaf_ref, oab_ref, ag_ref, ob_ref, oc_ref, odf_ref, odb_ref, dg_ref, gate_ref, x_ref, mod_ref,
                  wbr_ref, wout_ref, lng_ref, lnb_ref, wr_ref, br_ref, o_ref, g_ref):
    branches = (_rms_gate(oaf_ref[0] + oab_ref[0], ag_ref[0]), ob_ref[0], oc_ref[0],
                _rms_gate(odf_ref[0] + odb_ref[0], dg_ref[0]))
    acc = 0.0
    for i, br in enumerate(branches):
        y = _dot(br.astype(BF16), wbr_ref[i])
        acc = acc + _sigmoid(gate_ref[0, :, i * D_MODEL:(i + 1) * D_MODEL]) * y
    mix = _dot(acc.astype(BF16), wout_ref[...])
    m = mod_ref[0]
    z = ALPHA * x_ref[...] + m[2:3] * mix
    xn = _layer_norm(z, lng_ref[...], lnb_ref[...])
    o_ref[...] = xn
    h = xn * (1.0 + m[4:5]) + m[3:4]
    lane = lax.broadcasted_iota(jnp.int32, (h.shape[0], LANES), 1)
    gl = jnp.where(lane < N_GROUPS, _dot3(h, wr_ref[...]) + br_ref[...], -jnp.inf)
    gmax = jnp.max(gl, axis=-1, keepdims=True)
    g_ref[...] = jnp.min(jnp.where(gl == gmax, lane, LANES), axis=-1, keepdims=True)


def _merge_out_ln(o_a, o_b, o_c, o_d, p, x, mod, wbr, wout, lng, lnb, wr, br):
    b, l, _ = p.shape
    d = x.shape[1]
    tm = min(l, 256)
    nt = l // tm
    gw = N_BRANCH * D_MODEL
    bspec = pl.BlockSpec((1, tm, BRANCH_W), lambda bi, i: (bi, i, 0))

    def pcol(name):
        blk = OFF[name] // BRANCH_W
        return pl.BlockSpec((1, tm, BRANCH_W), lambda bi, i: (bi, i, blk))

    return pl.pallas_call(
        _merge_kernel,
        grid=(b, l // tm),
        in_specs=[bspec, bspec, pcol("a_g"), bspec, bspec, bspec, bspec, pcol("d_g"),
                  pl.BlockSpec((1, tm, gw), lambda bi, i: (bi, i, OFF["gate"] // gw)),
                  pl.BlockSpec((tm, d), lambda bi, i: (bi * nt + i, 0)),
                  pl.BlockSpec((1, 8, d), lambda bi, i: (bi, 0, 0)),
                  pl.BlockSpec((N_BRANCH, BRANCH_W, d), lambda bi, i: (0, 0, 0)),
                  pl.BlockSpec((d, d), lambda bi, i: (0, 0)),
                  pl.BlockSpec((1, d), lambda bi, i: (0, 0)),
                  pl.BlockSpec((1, d), lambda bi, i: (0, 0)),
                  pl.BlockSpec((d, LANES), lambda bi, i: (0, 0)),
                  pl.BlockSpec((1, LANES), lambda bi, i: (0, 0))],
        out_specs=[pl.BlockSpec((tm, d), lambda bi, i: (bi * nt + i, 0)),
                   pl.BlockSpec((tm, 1), lambda bi, i: (bi * nt + i, 0))],
        out_shape=[jax.ShapeDtypeStruct((b * l, d), F32), jax.ShapeDtypeStruct((b * l, 1), jnp.int32)],
        compiler_params=_cparams(("parallel", "parallel")),
        name="merge_out_ln",
    )(o_a[0], o_a[1], p, o_b, o_c, o_d[0], o_d[1], p, p, x, mod, wbr, wout, lng, lnb, wr, br)


def _moe_plan(gidx, n):
    tmax = n // MOE_TM + N_GROUPS
    i32 = jnp.int32
    key = gidx.reshape(n)
    tok = jnp.arange(n, dtype=i32)
    onehot = (key[:, None] == jnp.arange(N_GROUPS, dtype=i32)[None, :]).astype(i32)
    cum = jnp.cumsum(onehot, axis=0)
    counts = cum[-1]
    rank = jnp.sum(onehot * cum, axis=1) - 1
    tiles = (counts + MOE_TM - 1) // MOE_TM
    tile_end = jnp.cumsum(tiles)
    ntiles = tile_end[-1]
    row_start = (tile_end - tiles) * MOE_TM
    ppos = jnp.sum(onehot * row_start[None, :], axis=1) + rank
    placed = jnp.zeros((tmax * MOE_TM,), i32).at[ppos].set(tok).reshape(tmax, MOE_TM)
    last_tok = jnp.max(onehot * tok[:, None], axis=0)
    t = jnp.arange(tmax, dtype=i32)
    tc = jnp.minimum(t, ntiles - 1)
    seg = jnp.minimum(jnp.sum((tile_end[None, :] <= tc[:, None]).astype(i32), axis=1), N_GROUPS - 1)
    in_seg = tc - (tile_end[seg] - tiles[seg])
    off = in_seg[:, None] * MOE_TM + jnp.arange(MOE_TM, dtype=i32)[None, :]
    valid = jnp.logical_and(off < counts[seg][:, None], (t < ntiles)[:, None])
    rows_in = jnp.where(valid, placed, last_tok[seg][:, None])
    rows_out = jnp.where(valid, placed, n + jnp.arange(MOE_TM, dtype=i32)[None, :])
    return seg, rows_in, rows_out


def _moe_tile_kernel(tg_ref, rin_ref, rnext_ref, rout_ref, rvec_ref, x_hbm, mod_ref, wr_ref, br_ref, wg_ref,
                     wu_ref, wd_ref, lng_ref, lnb_ref, o_hbm, xbuf, ybuf, sem_in, sem_out, *, l):
    t = pl.program_id(0)
    last = pl.num_programs(0) - 1
    tm = MOE_TM
    slot = lax.rem(t, 2)

    def row_in(rows_ref, r, s):
        return pltpu.make_async_copy(x_hbm.at[pl.ds(rows_ref[0, 0, r], 1)], xbuf.at[s, pl.ds(r, 1)], sem_in.at[s])

    def row_out(r):
        return pltpu.make_async_copy(ybuf.at[pl.ds(r, 1)], o_hbm.at[pl.ds(rout_ref[0, 0, r], 1)], sem_out)

    @pl.when(t == 0)
    def _():
        for r in range(tm):
            row_in(rin_ref, r, 0).start()
        ybuf[...] = jnp.zeros_like(ybuf)
        spare = pltpu.make_async_copy(ybuf, o_hbm.at[pl.ds(o_hbm.shape[0] - tm, tm)], sem_out)
        spare.start()
        spare.wait()

    for r in range(tm):
        row_in(rin_ref, r, slot).wait()
    for r in range(tm):
        row_in(rnext_ref, r, 1 - slot).start()

    g = tg_ref[t]
    x = xbuf[slot]
    tok = rvec_ref[0]
    bidx = jnp.zeros((tm, 1), jnp.int32)
    nb = mod_ref.shape[0]
    for k in range(1, nb):
        bidx = bidx + (tok >= k * l).astype(jnp.int32)

    def mod_row(j):
        out = mod_ref[0, j:j + 1, :]
        for k in range(1, nb):
            out = jnp.where(bidx == k, mod_ref[k, j:j + 1, :], out)
        return out

    h = x * (1.0 + mod_row(4)) + mod_row(3)
    lane = lax.broadcasted_iota(jnp.int32, (tm, LANES), 1)
    ninf = -jnp.inf
    lg = _dot3(h, wr_ref[...]) + br_ref[...]
    gl = jnp.where(lane < N_GROUPS, lg, ninf)
    gmax = jnp.max(gl, axis=-1, keepdims=True)
    gsel = jnp.sum(jnp.where(lane == g, lg, 0.0), axis=-1, keepdims=True)
    gw = jnp.exp(gsel - gmax) / jnp.sum(jnp.exp(gl - gmax), axis=-1, keepdims=True)
    e0 = N_GROUPS + g * EXPERTS_PER_GROUP
    ingrp = jnp.logical_and(lane >= e0, lane < e0 + EXPERTS_PER_GROUP)
    e1 = jnp.where(ingrp, lg, ninf)
    m1 = jnp.max(e1, axis=-1, keepdims=True)
    i1 = jnp.min(jnp.where(e1 == m1, lane, LANES), axis=-1, keepdims=True)
    e2 = jnp.where(lane == i1, ninf, e1)
    m2 = jnp.max(e2, axis=-1, keepdims=True)
    i2 = jnp.min(jnp.where(e2 == m2, lane, LANES), axis=-1, keepdims=True)
    tt = jnp.exp(m2 - m1)
    wts = gw * (jnp.where(lane == i1, 1.0 / (1.0 + tt), 0.0) + jnp.where(lane == i2, tt / (1.0 + tt), 0.0))

    hb = h.astype(BF16)
    acc = jnp.zeros((tm, D_MODEL), F32)
    for j in range(EXPERTS_PER_GROUP):
        wcol = jnp.sum(jnp.where(lane == e0 + j, wts, 0.0), axis=-1, keepdims=True)
        hid = _silu(_dot(hb, wg_ref[j])) * _dot(hb, wu_ref[j]) * wcol
        acc = acc + _dot(hid.astype(BF16), wd_ref[j])
    y = _layer_norm(ALPHA * x + mod_row(5) * acc, lng_ref[...], lnb_ref[...])

    @pl.when(t > 0)
    def _():
        for r in range(tm):
            row_out(r).wait()

    ybuf[...] = y
    for r in range(tm):
        row_out(r).start()

    @pl.when(t == last)
    def _():
        for r in range(tm):
            row_out(r).wait()
        for r in range(tm):
            row_in(rnext_ref, r, 1 - slot).wait()


def _moe_ln(x, gidx, b, l, mod, wr, br, wg, wu, wd, lng, lnb):
    d = x.shape[1]
    n = b * l
    tile_group, rows_in, rows_out = _moe_plan(gidx, n)
    tmax = rows_in.shape[0]
    cst = lambda shape: pl.BlockSpec(shape, lambda t, tg: tuple(0 for _ in shape))
    smem_rows = lambda fn: pl.BlockSpec((1, 1, MOE_TM), fn, memory_space=pltpu.SMEM)
    grid_spec = pltpu.PrefetchScalarGridSpec(
        num_scalar_prefetch=1,
        grid=(tmax,),
        in_specs=[smem_rows(lambda t, tg: (t, 0, 0)),
                  smem_rows(lambda t, tg: (jnp.minimum(t + 1, tmax - 1), 0, 0)),
                  smem_rows(lambda t, tg: (t, 0, 0)),
                  pl.BlockSpec((1, MOE_TM, 1), lambda t, tg: (t, 0, 0)),
                  pl.BlockSpec(memory_space=pl.ANY),
                  cst(mod.shape), cst((d, LANES)), cst((1, LANES)),
                  pl.BlockSpec((EXPERTS_PER_GROUP, d, EXPERT_HIDDEN), lambda t, tg: (tg[t], 0, 0)),
                  pl.BlockSpec((EXPERTS_PER_GROUP, d, EXPERT_HIDDEN), lambda t, tg: (tg[t], 0, 0)),
                  pl.BlockSpec((EXPERTS_PER_GROUP, EXPERT_HIDDEN, d), lambda t, tg: (tg[t], 0, 0)),
                  cst((1, d)), cst((1, d))],
        out_specs=pl.BlockSpec(memory_space=pl.ANY),
        scratch_shapes=[pltpu.VMEM((2, MOE_TM, d), F32), pltpu.VMEM((MOE_TM, d), F32),
                        pltpu.SemaphoreType.DMA((2,)), pltpu.SemaphoreType.DMA],
    )
    rin3 = rows_in.reshape(tmax, 1, MOE_TM)
    return pl.pallas_call(
        functools.partial(_moe_tile_kernel, l=l),
        grid_spec=grid_spec,
        out_shape=jax.ShapeDtypeStruct((n + MOE_TM, d), F32),
        compiler_params=_cparams(("arbitrary",)),
        name="moe_ln",
    )(tile_group, rin3, rin3, rows_out.reshape(tmax, 1, MOE_TM), rows_in.reshape(tmax, MOE_TM, 1), x, mod, wr, br,
      wg, wu, wd, lng, lnb)


def _pad_cols(w, n):
    return jnp.pad(w, ((0, 0), (0, n - w.shape[1])))


def _wa_head_pairs(w, axis):
    shp = w.shape
    w = w.reshape(shp[:axis] + (2, WA_HEADS // 2, WA_DH) + shp[axis + 1:])
    w = jnp.swapaxes(w, axis, axis + 1)
    return w.reshape(shp)


def _prep_w_in(w):
    segs = []
    for name in _NEW_ORDER:
        s = w[:, _ORIG_OFF[name]:_ORIG_OFF[name] + _SIZES[name]]
        if name == "c_q":
            s = _wa_head_pairs(s, 1)
        segs.append(s)
    segs.append(jnp.zeros((w.shape[0], NP - D_IN), w.dtype))
    return jnp.concatenate(segs, 1).astype(BF16)


def _prep_gla_gate(lr_w, lr_b):
    w = jnp.zeros((2, 2, LANES, LANES), F32)
    for z in range(2):
        for n in range(2):
            w = w.at[z, n, z * GLA_RANK:(z + 1) * GLA_RANK, :].set(lr_w[z][:, n * LANES:(n + 1) * LANES])
    return w.astype(BF16), lr_b.reshape(2, 2, 1, LANES)


def _sink_rows(sink, rows_per_head):
    s = sink.reshape(2, 4)
    return jnp.broadcast_to(s[:, :, None, None], (2, 4, rows_per_head, LANES)).reshape(2, 4 * rows_per_head, LANES)


def kernel(x, c, ctx, c_ctx, w_mod, b_mod, w_in, gla_lr_w, gla_lr_b, hg_gamma, na_rpb, wa_sink, w_branch, w_out,
           ln_g, ln_b, moe_w_group, moe_b_group, moe_w_expert, moe_b_expert, moe_w_gate, moe_w_up, moe_w_down):
    b, l, d = x.shape
    lc = ctx.shape[1]
    assert d == D_MODEL and b + 1 <= 8 and l % 512 == 0 and l >= 3 * WA_BLOCK and lc % SCAN_C == 0

    cc = jnp.concatenate([c, c_ctx[None], jnp.zeros((8 - b - 1, d), F32)], 0)
    mods = _modulation(cc, w_mod, b_mod)
    cum = jnp.cumsum(jax.nn.softmax(hg_gamma.astype(F32), axis=1), axis=1)
    lower_bounds = cum - cum[:, :1]
    rope_tabs = _rope_tables(l)

    x = x.reshape(b * l, d)
    cx = ctx.reshape(b * lc, d)
    for li in range(DEPTH):
        with_ctx = li < DEPTH - 1
        mod6 = mods[li].reshape(8, 6, d)
        mod = jnp.pad(mod6[:b], ((0, 0), (0, 2), (0, 0)))
        modc = jnp.broadcast_to(jnp.pad(mod6[b], ((0, 2), (0, 0)))[None], (b, 8, d))
        w_in_p = _prep_w_in(w_in[li])
        p = _in_proj(x, b, l, mod, w_in_p)
        pc = _in_proj(cx, b, lc, modc, w_in_p)

        gla_aux = _prep_gla_gate(gla_lr_w[li], gla_lr_b[li])
        o_a, oc_a = _recurrent_branch("gla", p, pc, gla_aux)
        hg_aux = (lower_bounds[:, li].reshape(2, 1, HG_HEADS * HG_DK),)
        o_d, oc_d = _recurrent_branch("hgrn", p, pc, hg_aux)

        o_b = _na_attention(p, pc, _na_bias_table(na_rpb[li]))
        qr, kr = _rope(p, rope_tabs)
        o_c = _wa_attention(qr, kr, p, pc, _sink_rows(wa_sink[li], WA_BLOCK))

        wbr = jnp.stack([w_branch[li, 0], w_branch[li, 1], _wa_head_pairs(w_branch[li, 2], 0),
                         w_branch[li, 3]]).astype(BF16)
        wout = w_out[li].astype(BF16)
        lng0, lnb0 = ln_g[li, 0][None], ln_b[li, 0][None]
        wr = _pad_cols(jnp.concatenate([moe_w_group[li], moe_w_expert[li]], 1), LANES)
        br = _pad_cols(jnp.concatenate([moe_b_group[li], moe_b_expert[li]])[None], LANES)
        x, gidx = _merge_out_ln(o_a, o_b, o_c, o_d, p, x, mod, wbr, wout, lng0, lnb0, wr, br)

        wg = moe_w_gate[li].reshape(N_EXPERTS, d, EXPERT_HIDDEN).astype(BF16)
        wu = moe_w_up[li].reshape(N_EXPERTS, d, EXPERT_HIDDEN).astype(BF16)
        wd = moe_w_down[li].reshape(N_EXPERTS, EXPERT_HIDDEN, d).astype(BF16)
        lng1, lnb1 = ln_g[li, 1][None], ln_b[li, 1][None]
        moe = functools.partial(_moe_ln, wr=wr, br=br, wg=wg, wu=wu, wd=wd, lng=lng1, lnb=lnb1)
        if with_ctx:
            oc_b = _ctx_attention(pc, "b_q", "b_k", "b_v", True, None)
            oc_c = _ctx_attention(pc, "c_q", "c_k", "c_v", False,
                                  jnp.broadcast_to(wa_sink[li].reshape(2, 4).T[:, :, None, None], (4, 2, lc, LANES)))
            cx, gidx_c = _merge_out_ln(oc_a, oc_b, oc_c, oc_d, pc, cx, modc, wbr, wout, lng0, lnb0, wr, br)
            cx = moe(cx, gidx_c, b, lc, modc)
        x = moe(x, gidx, b, l, mod)
    return x[:b * l].reshape(b, l, d)
```
